```python
import math
import jax, jax.numpy as jnp
from jax import lax
import numpy as np

D_MODEL = 1024
BATCH = 32
SEQ = 2048
DEPTH = 2
DEC_BATCH = 4
DEC_SEQ = 4096
PAST_LEN = 128

EPS = 1e-6
NEG_INF = -1e30
Q_BLOCK = 128
CONV_CH = 256
CONV_WIDTH = 31
SGU_CH = 256
SGU_GROUPS = 4
SGU_CHUNK = 128
MLA_HEADS = 4
MLA_Q_RANK = 256
MLA_KV_RANK = 128
MLA_NOPE = 64
MLA_ROPE = 32
MLA_V = 64
MLA_THETA = 10000.0
DIL_GROUPS = ((128, 1), (512, 4), (2048, 16))
N_DIL = 3
DIL_HEADS = 4
DIL_HEAD_DIM = 64
ROPE_THETA = 500000.0
ROPE_DIMS = DIL_HEAD_DIM // 4
N_BRANCH = 4
D_FF = 2816
FFN_CONV_WIDTH = 3
N_A = 2 * CONV_CH
N_B = 2 * SGU_CH
N_C = MLA_Q_RANK + MLA_KV_RANK + MLA_ROPE
N_D = 3 * N_DIL * DIL_HEADS * DIL_HEAD_DIM
N_G = N_BRANCH * D_MODEL
N_IN = N_A + N_B + N_C + N_D + N_G
IN_SPLITS = (N_A, N_A + N_B, N_A + N_B + N_C, N_A + N_B + N_C + N_D)

kernel_name = 'hybrid_gated_encoder'


def rms_norm(x, g):
    xf = x.astype(jnp.float32)
    y = xf * lax.rsqrt(jnp.mean(xf * xf, axis=-1, keepdims=True) + EPS)
    return (y * g.astype(jnp.float32)).astype(x.dtype)


def layer_norm(x, g, b):
    xf = x.astype(jnp.float32)
    mu = jnp.mean(xf, axis=-1, keepdims=True)
    var = jnp.mean(jnp.square(xf - mu), axis=-1, keepdims=True)
    y = (xf - mu) * lax.rsqrt(var + EPS) * g.astype(jnp.float32) + b.astype(jnp.float32)
    return y.astype(x.dtype)


def depthwise_conv(x, w, b):
    k = w.shape[0]
    y = lax.conv_general_dilated(x, w[:, None, :].astype(x.dtype), window_strides=(1,),
                                 padding=[(k // 2, k // 2)],
                                 dimension_numbers=('NWC', 'WIO', 'NWC'),
                                 feature_group_count=x.shape[-1])
    return y + b.astype(x.dtype)


def rope_tables(seq, dims, theta):
    inv = jnp.exp(-math.log(theta) * jnp.arange(0, dims, 2, dtype=jnp.float32) / dims)
    ang = jnp.arange(seq, dtype=jnp.float32)[:, None] * inv[None, :]
    return jnp.cos(ang), jnp.sin(ang)


def apply_rope(x, cos, sin):
    half = x.shape[-1] // 2
    bshape = (1, cos.shape[0]) + (1,) * (x.ndim - 3) + (half,)
    c = cos.reshape(bshape)
    s = sin.reshape(bshape)
    xf = x.astype(jnp.float32)
    x1, x2 = xf[..., :half], xf[..., half:]
    return jnp.concatenate([x1 * c - x2 * s, x1 * s + x2 * c], axis=-1).astype(x.dtype)


def conv_module(z, w_dw, b_dw, ln_g, ln_b, w_o):
    a, g = jnp.split(z, 2, axis=-1)
    h = a * jax.nn.sigmoid(g)
    h = depthwise_conv(h, w_dw, b_dw)
    h = jax.nn.silu(layer_norm(h, ln_g, ln_b))
    return h @ w_o


def spatial_gating(z, ln_g, ln_b, w_s, b_s, w_o):
    B, S, _ = z.shape
    z = jax.nn.gelu(z)
    u, v = jnp.split(z, 2, axis=-1)
    v = layer_norm(v, ln_g, ln_b)
    v = v.reshape(B, S // SGU_CHUNK, SGU_CHUNK, SGU_GROUPS, SGU_CH // SGU_GROUPS)
    sv = jnp.einsum('gts,bcsgd->bctgd', w_s.astype(v.dtype), v)
    sv = sv + b_s.T.astype(v.dtype)[None, None, :, :, None]
    y = u * sv.reshape(B, S, SGU_CH)
    return y @ w_o


def mla(z, g_cq, g_ckv, w_uq, w_ukv, g_qn, g_kn, w_o, cos, sin):
    B, S, _ = z.shape
    dqk = MLA_NOPE + MLA_ROPE
    c_q, c_kv, k_r = jnp.split(z, [MLA_Q_RANK, MLA_Q_RANK + MLA_KV_RANK], axis=-1)
    q = (rms_norm(c_q, g_cq) @ w_uq).reshape(B, S, MLA_HEADS, dqk)
    kv = (rms_norm(c_kv, g_ckv) @ w_ukv).reshape(B, S, MLA_HEADS, MLA_NOPE + MLA_V)
    k_nope, v = jnp.split(kv, [MLA_NOPE], axis=-1)
    k = jnp.concatenate([k_nope, jnp.broadcast_to(k_r[:, :, None, :], (B, S, MLA_HEADS, MLA_ROPE))], axis=-1)
    q = rms_norm(q, g_qn)
    k = rms_norm(k, g_kn)
    q = jnp.concatenate([q[..., :MLA_NOPE], apply_rope(q[..., MLA_NOPE:], cos, sin)], axis=-1)
    k = jnp.concatenate([k[..., :MLA_NOPE], apply_rope(k[..., MLA_NOPE:], cos, sin)], axis=-1)
    q = q * (dqk ** -0.5)
    nb = S // Q_BLOCK
    qb = q.reshape(B, nb, Q_BLOCK, MLA_HEADS, dqk).transpose(1, 0, 2, 3, 4)

    def block(qi):
        s = jnp.einsum('bqhd,bkhd->bhqk', qi, k, preferred_element_type=jnp.float32)
        p = jax.nn.softmax(s, axis=-1)
        return jnp.einsum('bhqk,bkhd->bqhd', p.astype(v.dtype), v)

    o = lax.map(block, qb)
    o = o.transpose(1, 0, 2, 3, 4).reshape(B, S, MLA_HEADS * MLA_V)
    return o @ w_o


def dilated_attention(z, g_qn, g_kn, w_o, cos, sin):
    B, S, _ = z.shape
    q, k, v = jnp.split(z, 3, axis=-1)
    shp = (B, S, N_DIL, DIL_HEADS, DIL_HEAD_DIM)
    q, k, v = q.reshape(shp), k.reshape(shp), v.reshape(shp)
    q = rms_norm(q, g_qn)
    k = rms_norm(k, g_kn)
    q = jnp.concatenate([apply_rope(q[..., :ROPE_DIMS], cos, sin), q[..., ROPE_DIMS:]], axis=-1)
    k = jnp.concatenate([apply_rope(k[..., :ROPE_DIMS], cos, sin), k[..., ROPE_DIMS:]], axis=-1)
    q = q * (DIL_HEAD_DIM ** -0.5)
    k_groups = [k[:, :, g] for g in range(N_DIL)]
    v_groups = [v[:, :, g] for g in range(N_DIL)]
    nb = S // Q_BLOCK
    qb = q.reshape(B, nb, Q_BLOCK, N_DIL, DIL_HEADS, DIL_HEAD_DIM).transpose(1, 0, 2, 3, 4, 5)

    def block(args):
        bi, qi = args
        pos = bi * Q_BLOCK + jnp.arange(Q_BLOCK, dtype=jnp.int32)
        outs, lses = [], []
        for g, (win, dil) in enumerate(DIL_GROUPS):
            n_side = (win // 2) // dil
            offs = dil * jnp.arange(-n_side, n_side + 1, dtype=jnp.int32)
            idx = pos[:, None] + offs[None, :]
            valid = (idx >= 0) & (idx < S)
            idx = jnp.clip(idx, 0, S - 1)
            kg = jnp.take(k_groups[g], idx, axis=1)
            vg = jnp.take(v_groups[g], idx, axis=1)
            s = jnp.einsum('bqhd,bqjhd->bhqj', qi[:, :, g], kg, preferred_element_type=jnp.float32)
            s = jnp.where(valid[None, None], s, NEG_INF)
            lse = jax.nn.logsumexp(s, axis=-1)
            p = jnp.exp(s - lse[..., None])
            outs.append(jnp.einsum('bhqj,bqjhd->bqhd', p.astype(vg.dtype), vg))
            lses.append(lse)
        wts = jax.nn.softmax(jnp.stack(lses, axis=0), axis=0)
        wts = wts.transpose(0, 1, 3, 2)[..., None]
        o = outs[0] * wts[0].astype(outs[0].dtype)
        for g in range(1, N_DIL):
            o = o + outs[g] * wts[g].astype(outs[g].dtype)
        return o

    o = lax.map(block, (jnp.arange(nb, dtype=jnp.int32), qb))
    o = o.transpose(1, 0, 2, 3, 4).reshape(B, S, DIL_HEADS * DIL_HEAD_DIM)
    return o @ w_o


def conv_ffn(h, w_up, w_dw, b_dw, w_down):
    u = depthwise_conv(h @ w_up, w_dw, b_dw)
    a, b = jnp.split(u, 2, axis=-1)
    return (jax.nn.silu(a) * b) @ w_down


def _trunk(x, weights):
    (attn_norm, w_in, conv_w, conv_b, conv_ln_g, conv_ln_b, conv_w_o,
     sgu_ln_g, sgu_ln_b, sgu_w_s, sgu_b_s, sgu_w_o,
     mla_g_cq, mla_g_ckv, mla_w_uq, mla_w_ukv, mla_g_qn, mla_g_kn, mla_w_o,
     dil_g_qn, dil_g_kn, dil_w_o, w_out,
     ffn_norm, ffn_w_up, ffn_conv_w, ffn_conv_b, ffn_w_down) = weights
    B, S, _ = x.shape
    cos_c, sin_c = rope_tables(S, MLA_ROPE, MLA_THETA)
    cos_d, sin_d = rope_tables(S, ROPE_DIMS, ROPE_THETA)
    for l in range(DEPTH):
        h = rms_norm(x, attn_norm[l])
        z = h @ w_in[l]
        z_a, z_b, z_c, z_d, z_g = jnp.split(z, IN_SPLITS, axis=-1)
        gates = jax.nn.sigmoid(z_g.reshape(B, S, N_BRANCH, D_MODEL))
        y_a = conv_module(z_a, conv_w[l], conv_b[l], conv_ln_g[l], conv_ln_b[l], conv_w_o[l])
        y_b = spatial_gating(z_b, sgu_ln_g[l], sgu_ln_b[l], sgu_w_s[l], sgu_b_s[l], sgu_w_o[l])
        y_c = mla(z_c, mla_g_cq[l], mla_g_ckv[l], mla_w_uq[l], mla_w_ukv[l], mla_g_qn[l], mla_g_kn[l],
                  mla_w_o[l], cos_c, sin_c)
        y_d = dilated_attention(z_d, dil_g_qn[l], dil_g_kn[l], dil_w_o[l], cos_d, sin_d)
        merged = (gates[:, :, 0] * y_a + gates[:, :, 1] * y_b
                  + gates[:, :, 2] * y_c + gates[:, :, 3] * y_d)
        x = x + merged @ w_out[l]
        x = x + conv_ffn(rms_norm(x, ffn_norm[l]), ffn_w_up[l], ffn_conv_w[l], ffn_conv_b[l], ffn_w_down[l])
    return x


def setup_inputs(seed: int = 0) -> dict:
    key = jax.random.key(seed)
    ks = iter(jax.random.split(key, 40))
    L = DEPTH

    def nrm(shape, scale):
        return scale * jax.random.normal(next(ks), shape, jnp.float32)

    def gain(shape):
        return 1.0 + 0.05 * jax.random.normal(next(ks), shape, jnp.float32)

    return {
        'x_prompt': nrm((BATCH, SEQ, D_MODEL), 1.0),
        'x_sample': nrm((DEC_BATCH, DEC_SEQ, D_MODEL), 1.0),
        'attn_norm': gain((L, D_MODEL)),
        'w_in': nrm((L, D_MODEL, N_IN), D_MODEL ** -0.5),
        'conv_w': nrm((L, CONV_WIDTH, CONV_CH), CONV_WIDTH ** -0.5),
        'conv_b': nrm((L, CONV_CH), 0.02),
        'conv_ln_g': gain((L, CONV_CH)),
        'conv_ln_b': nrm((L, CONV_CH), 0.02),
        'conv_w_o': nrm((L, CONV_CH, D_MODEL), CONV_CH ** -0.5),
        'sgu_ln_g': gain((L, SGU_CH)),
        'sgu_ln_b': nrm((L, SGU_CH), 0.02),
        'sgu_w_s': nrm((L, SGU_GROUPS, SGU_CHUNK, SGU_CHUNK), SGU_CHUNK ** -0.5),
        'sgu_b_s': gain((L, SGU_GROUPS, SGU_CHUNK)),
        'sgu_w_o': nrm((L, SGU_CH, D_MODEL), SGU_CH ** -0.5),
        'mla_g_cq': gain((L, MLA_Q_RANK)),
        'mla_g_ckv': gain((L, MLA_KV_RANK)),
        'mla_w_uq': nrm((L, MLA_Q_RANK, MLA_HEADS * (MLA_NOPE + MLA_ROPE)), MLA_Q_RANK ** -0.5),
        'mla_w_ukv': nrm((L, MLA_KV_RANK, MLA_HEADS * (MLA_NOPE + MLA_V)), MLA_KV_RANK ** -0.5),
        'mla_g_qn': gain((L, MLA_NOPE + MLA_ROPE)),
        'mla_g_kn': gain((L, MLA_NOPE + MLA_ROPE)),
        'mla_w_o': nrm((L, MLA_HEADS * MLA_V, D_MODEL), (MLA_HEADS * MLA_V) ** -0.5),
        'dil_g_qn': gain((L, DIL_HEAD_DIM)),
        'dil_g_kn': gain((L, DIL_HEAD_DIM)),
        'dil_w_o': nrm((L, DIL_HEADS * DIL_HEAD_DIM, D_MODEL), (DIL_HEADS * DIL_HEAD_DIM) ** -0.5),
        'w_out': nrm((L, D_MODEL, D_MODEL), D_MODEL ** -0.5),
        'ffn_norm': gain((L, D_MODEL)),
        'ffn_w_up': nrm((L, D_MODEL, 2 * D_FF), D_MODEL ** -0.5),
        'ffn_conv_w': nrm((L, FFN_CONV_WIDTH, 2 * D_FF), FFN_CONV_WIDTH ** -0.5),
        'ffn_conv_b': nrm((L, 2 * D_FF), 0.02),
        'ffn_w_down': nrm((L, D_FF, D_MODEL), D_FF ** -0.5),
    }


def reference(x_prompt, x_sample, attn_norm, w_in, conv_w, conv_b, conv_ln_g, conv_ln_b, conv_w_o,
              sgu_ln_g, sgu_ln_b, sgu_w_s, sgu_b_s, sgu_w_o,
              mla_g_cq, mla_g_ckv, mla_w_uq, mla_w_ukv, mla_g_qn, mla_g_kn, mla_w_o,
              dil_g_qn, dil_g_kn, dil_w_o, w_out,
              ffn_norm, ffn_w_up, ffn_conv_w, ffn_conv_b, ffn_w_down):
    weights = (attn_norm, w_in, conv_w, conv_b, conv_ln_g, conv_ln_b, conv_w_o,
               sgu_ln_g, sgu_ln_b, sgu_w_s, sgu_b_s, sgu_w_o,
               mla_g_cq, mla_g_ckv, mla_w_uq, mla_w_ukv, mla_g_qn, mla_g_kn, mla_w_o,
               dil_g_qn, dil_g_kn, dil_w_o, w_out,
               ffn_norm, ffn_w_up, ffn_conv_w, ffn_conv_b, ffn_w_down)
    y_prompt = _trunk(x_prompt, weights)
    y_sample = _trunk(x_sample, weights)
    return (y_prompt, y_sample)
```

```python
import functools
import math

import jax
import jax.numpy as jnp
from jax import lax
from jax.experimental import pallas as pl
from jax.experimental.pallas import tpu as pltpu

F32 = jnp.float32
BF16 = jnp.bfloat16

D_MODEL = 1024
DEPTH = 2
EPS = 1e-6
NEG_INF = -1e30
CONV_CH = 256
CONV_WIDTH = 31
SGU_CH = 256
SGU_GROUPS = 4
SGU_CHUNK = 128
MLA_HEADS = 4
MLA_Q_RANK = 256
MLA_KV_RANK = 128
MLA_NOPE = 64
MLA_ROPE = 32
MLA_V = 64
MLA_THETA = 10000.0
DIL_GROUPS = ((128, 1), (512, 4), (2048, 16))
N_DIL = 3
DIL_HEADS = 4
DIL_HEAD_DIM = 64
ROPE_THETA = 500000.0
ROPE_DIMS = DIL_HEAD_DIM // 4
N_BRANCH = 4
D_FF = 2816
N_A = 2 * CONV_CH
N_B = 2 * SGU_CH
N_C = MLA_Q_RANK + MLA_KV_RANK + MLA_ROPE
N_D = 3 * N_DIL * DIL_HEADS * DIL_HEAD_DIM
DIL_SIDE = 64
MLA_DQK = MLA_NOPE + MLA_ROPE
DIL_W = DIL_HEADS * DIL_HEAD_DIM

LANES = 128
SUBLANES = 8
VMEM_LIMIT = 56 * 1024 * 1024

TM_PROJ = 256
TM_CONV = 512
CONV_HALO = 16
TQ_MLA = 256
TK_MLA = 512
TQ_DIL = 256
TM_MERGE = 256
TM_FFN = 512
FFN_HALO = 16
FFN_CHUNK = 256


def _const_spec(shape):
    nd = len(shape)
    return pl.BlockSpec(shape, lambda *_: (0,) * nd, pipeline_mode=pl.Buffered(1))


def _sigmoid(x):
    return 1.0 / (1.0 + jnp.exp(-x))


def _rms_scale(x, n):
    return lax.rsqrt(jnp.sum(x * x, axis=-1, keepdims=True) * (1.0 / n) + EPS)


def _layer_norm(x, g, b):
    mu = jnp.mean(x, axis=-1, keepdims=True)
    xc = x - mu
    var = jnp.mean(xc * xc, axis=-1, keepdims=True)
    return xc * lax.rsqrt(var + EPS) * g + b


def _rope(x, cos, sa, sb, half):
    return x * cos + pltpu.roll(x, LANES - half, 1) * sa + pltpu.roll(x, half, 1) * sb


def _inproj_kernel(x_ref, mcos_ref, msa_ref, msb_ref, dcos_ref, dsa_ref, dsb_ref,
                   g_attn_ref, w_ab_ref, w_c_ref, w_d_ref,
                   sgu_g_ref, sgu_b_ref, ws_ref, bs_ref,
                   g_cq_ref, g_ckv_ref, w_uq_ref, w_uk_ref, w_uv_ref, gq_mla_ref, gk_mla_ref,
                   gq_dil_ref, gk_dil_ref,
                   ha_ref, yb_ref, qm_ref, km_ref, vm_ref, qd_ref, kd_ref, vd_ref):
    tm = x_ref.shape[0]
    x = x_ref[...]
    h = (x * _rms_scale(x, D_MODEL) * g_attn_ref[...]).astype(BF16)

    z_ab = jnp.dot(h, w_ab_ref[...], preferred_element_type=F32)
    ha_ref[...] = z_ab[:, :CONV_CH] * _sigmoid(z_ab[:, CONV_CH:N_A])
    zb = z_ab[:, N_A:]
    zb = 0.5 * zb * (1.0 + jnp.tanh(math.sqrt(2.0 / math.pi) * (zb + 0.044715 * (zb * zb * zb))))
    u = zb[:, :SGU_CH]
    v = _layer_norm(zb[:, SGU_CH:], sgu_g_ref[...], sgu_b_ref[...]).astype(BF16)
    lane = lax.broadcasted_iota(jnp.int32, (SGU_CHUNK, SGU_CH), 1)
    gw = SGU_CH // SGU_GROUPS
    for c in range(tm // SGU_CHUNK):
        rows = slice(c * SGU_CHUNK, (c + 1) * SGU_CHUNK)
        vc = v[rows]
        sv = jnp.dot(ws_ref[SGU_GROUPS - 1], vc, preferred_element_type=F32)
        for g in range(SGU_GROUPS - 2, -1, -1):
            sv = jnp.where(lane < (g + 1) * gw, jnp.dot(ws_ref[g], vc, preferred_element_type=F32), sv)
        yb_ref[rows, :] = (u[rows] * (sv + bs_ref[...])).astype(BF16)

    z_c = jnp.dot(h, w_c_ref[...], preferred_element_type=F32)
    c_q = z_c[:, :MLA_Q_RANK]
    c_q = (c_q * _rms_scale(c_q, MLA_Q_RANK) * g_cq_ref[...]).astype(BF16)
    c_kv = z_c[:, MLA_Q_RANK:MLA_Q_RANK + MLA_KV_RANK]
    c_kv = (c_kv * _rms_scale(c_kv, MLA_KV_RANK) * g_ckv_ref[...]).astype(BF16)
    k_rope = z_c[:, MLA_Q_RANK + MLA_KV_RANK:]
    q = jnp.dot(c_q, w_uq_ref[...], preferred_element_type=F32)
    kn = jnp.dot(c_kv, w_uk_ref[...], preferred_element_type=F32)
    vm_ref[...] = jnp.dot(c_kv, w_uv_ref[...], preferred_element_type=F32).astype(BF16)
    mcos, msa, msb = mcos_ref[...], msa_ref[...], msb_ref[...]
    for hd in range(MLA_HEADS):
        cols = slice(hd * LANES, (hd + 1) * LANES)
        qh = q[:, cols]
        qh = qh * _rms_scale(qh, MLA_DQK) * gq_mla_ref[...]
        qm_ref[:, cols] = _rope(qh, mcos, msa, msb, MLA_ROPE // 2).astype(BF16)
        kh = kn[:, cols] + k_rope
        kh = kh * _rms_scale(kh, MLA_DQK) * gk_mla_ref[...]
        km_ref[:, cols] = _rope(kh, mcos, msa, msb, MLA_ROPE // 2).astype(BF16)

    z_d = jnp.dot(h, w_d_ref[...], preferred_element_type=F32)
    n_qk = N_DIL * DIL_W
    vd_ref[...] = z_d[:, 2 * n_qk:].astype(BF16)
    dcos, dsa, dsb = dcos_ref[...], dsa_ref[...], dsb_ref[...]
    low = lax.broadcasted_iota(jnp.int32, (tm, LANES), 1) < DIL_HEAD_DIM
    for which, (gain_ref, out_ref) in enumerate(((gq_dil_ref, qd_ref), (gk_dil_ref, kd_ref))):
        for c in range(n_qk // LANES):
            xc = z_d[:, which * n_qk + c * LANES: which * n_qk + (c + 1) * LANES]
            sq = xc * xc
            s_lo = jnp.sum(jnp.where(low, sq, 0.0), axis=-1, keepdims=True)
            s_hi = jnp.sum(jnp.where(low, 0.0, sq), axis=-1, keepdims=True)
            ms = jnp.where(low, s_lo, s_hi) * (1.0 / DIL_HEAD_DIM)
            xn = xc * lax.rsqrt(ms + EPS) * gain_ref[...]
            out_ref[:, c * LANES:(c + 1) * LANES] = _rope(xn, dcos, dsa, dsb, ROPE_DIMS // 2).astype(BF16)


def _inproj(x, tabs, lw):
    B, S, _ = x.shape
    tm = TM_PROJ
    grid = (B, S // tm)
    tok = lambda w, dt: jax.ShapeDtypeStruct((B, S, w), dt)
    tok_spec = lambda w: pl.BlockSpec((None, tm, w), lambda b, i: (b, i, 0))
    tab_spec = pl.BlockSpec((tm, LANES), lambda b, i: (i, 0))
    weights = (lw['g_attn'], lw['w_ab'], lw['w_c'], lw['w_d'], lw['sgu_g'], lw['sgu_b'], lw['ws'], lw['bs'],
               lw['g_cq'], lw['g_ckv'], lw['w_uq'], lw['w_uk'], lw['w_uv'], lw['gq_mla'], lw['gk_mla'],
               lw['gq_dil'], lw['gk_dil'])
    return pl.pallas_call(
        _inproj_kernel,
        grid=grid,
        in_specs=[tok_spec(D_MODEL)] + [tab_spec] * 6 + [_const_spec(w.shape) for w in weights],
        out_specs=[tok_spec(CONV_CH), tok_spec(SGU_CH), tok_spec(4 * LANES), tok_spec(4 * LANES), tok_spec(4 * LANES),
                   tok_spec(N_DIL * DIL_W), tok_spec(N_DIL * DIL_W), tok_spec(N_DIL * DIL_W)],
        out_shape=[tok(CONV_CH, F32), tok(SGU_CH, BF16), tok(4 * LANES, BF16), tok(4 * LANES, BF16), tok(4 * LANES, BF16),
                   tok(N_DIL * DIL_W, BF16), tok(N_DIL * DIL_W, BF16), tok(N_DIL * DIL_W, BF16)],
        compiler_params=pltpu.CompilerParams(dimension_semantics=("parallel", "parallel"),
                                             vmem_limit_bytes=VMEM_LIMIT),
        name="inproj",
    )(x, *tabs, *weights)


def _conv_kernel(prev_ref, cur_ref, next_ref, w_ref, b_ref, g_ref, beta_ref, o_ref, ext_ref):
    tm = cur_ref.shape[0]
    i = pl.program_id(1)
    last = pl.num_programs(1) - 1
    ext_ref[0:CONV_HALO, :] = jnp.where(i > 0, prev_ref[...], 0.0)
    ext_ref[CONV_HALO:CONV_HALO + tm, :] = cur_ref[...]
    ext_ref[CONV_HALO + tm:, :] = jnp.where(i < last, next_ref[...], 0.0)
    acc = jnp.zeros((tm, CONV_CH), F32) + b_ref[...]
    base = CONV_HALO - CONV_WIDTH // 2
    for k in range(CONV_WIDTH):
        acc = acc + ext_ref[base + k:base + k + tm, :] * w_ref[k:k + 1, :]
    y = _layer_norm(acc, g_ref[...], beta_ref[...])
    o_ref[...] = (y * _sigmoid(y)).astype(BF16)


def _conv_module(ha, lw):
    B, S, _ = ha.shape
    tm = TM_CONV
    nh = tm // CONV_HALO
    n_halo_blocks = S // CONV_HALO
    weights = (lw['conv_w'], lw['conv_b'], lw['conv_g'], lw['conv_beta'])
    return pl.pallas_call(
        _conv_kernel,
        grid=(B, S // tm),
        in_specs=[pl.BlockSpec((None, CONV_HALO, CONV_CH), lambda b, i: (b, jnp.maximum(i * nh - 1, 0), 0)),
                  pl.BlockSpec((None, tm, CONV_CH), lambda b, i: (b, i, 0)),
                  pl.BlockSpec((None, CONV_HALO, CONV_CH),
                               lambda b, i: (b, jnp.minimum((i + 1) * nh, n_halo_blocks - 1), 0))]
                 + [_const_spec(w.shape) for w in weights],
        out_specs=pl.BlockSpec((None, tm, CONV_CH), lambda b, i: (b, i, 0)),
        out_shape=jax.ShapeDtypeStruct((B, S, CONV_CH), BF16),
        scratch_shapes=[pltpu.VMEM((tm + 2 * CONV_HALO, CONV_CH), F32)],
        compiler_params=pltpu.CompilerParams(dimension_semantics=("parallel", "parallel"),
                                             vmem_limit_bytes=VMEM_LIMIT),
        name="conv_module",
    )(ha, ha, ha, *weights)


def _mla_kernel(q_ref, k_ref, v_ref, o_ref, m_ref, l_ref, acc_ref, *, tk):
    tq = q_ref.shape[0]
    n_kv = k_ref.shape[0] // tk
    m_ref[...] = jnp.full(m_ref.shape, NEG_INF, F32)
    l_ref[...] = jnp.zeros(l_ref.shape, F32)
    acc_ref[...] = jnp.zeros(acc_ref.shape, F32)

    def body(j, carry):
        ks = pl.multiple_of(j * tk, tk)
        for hd in range(MLA_HEADS):
            cols = slice(hd * LANES, (hd + 1) * LANES)
            s = lax.dot_general(q_ref[:, cols], k_ref[pl.ds(ks, tk), cols], (((1,), (1,)), ((), ())),
                                preferred_element_type=F32)
            m_old = m_ref[hd]
            m_new = jnp.maximum(m_old, jnp.max(s, axis=-1, keepdims=True))
            alpha = jnp.exp(m_old - m_new)
            p = jnp.exp(s - m_new)
            l_ref[hd] = alpha * l_ref[hd] + jnp.sum(p, axis=-1, keepdims=True)
            acc_ref[hd] = alpha * acc_ref[hd] + jnp.dot(p.astype(BF16), v_ref[pl.ds(ks, tk), cols],
                                                         preferred_element_type=F32)
            m_ref[hd] = m_new
        return carry

    lax.fori_loop(0, n_kv, body, 0)
    for hd in range(MLA_HEADS):
        o_ref[:, hd * LANES:(hd + 1) * LANES] = (acc_ref[hd] / l_ref[hd]).astype(BF16)


def _mla_attention(q, k, v):
    B, S, W = q.shape
    tq = TQ_MLA
    return pl.pallas_call(
        functools.partial(_mla_kernel, tk=TK_MLA),
        grid=(B, S // tq),
        in_specs=[pl.BlockSpec((None, tq, W), lambda b, i: (b, i, 0)),
                  pl.BlockSpec((None, S, W), lambda b, i: (b, 0, 0)),
                  pl.BlockSpec((None, S, W), lambda b, i: (b, 0, 0))],
        out_specs=pl.BlockSpec((None, tq, W), lambda b, i: (b, i, 0)),
        out_shape=jax.ShapeDtypeStruct((B, S, W), BF16),
        scratch_shapes=[pltpu.VMEM((MLA_HEADS, tq, 1), F32), pltpu.VMEM((MLA_HEADS, tq, 1), F32),
                        pltpu.VMEM((MLA_HEADS, tq, LANES), F32)],
        compiler_params=pltpu.CompilerParams(dimension_semantics=("parallel", "arbitrary"),
                                             vmem_limit_bytes=VMEM_LIMIT),
        name="mla_attention",
    )(q, k, v)


def _dil_kernel(q_ref, k_ref, v_ref, o_ref, lse_ref, *, tk):
    tq = q_ref.shape[0]
    n = k_ref.shape[0]
    i0 = pl.program_id(2) * tq
    if tk == n:
        start = 0
        kw, vw = k_ref[...], v_ref[...]
    else:
        start = pl.multiple_of(jnp.clip(i0 - DIL_SIDE, 0, n - tk), DIL_SIDE)
        kw, vw = k_ref[pl.ds(start, tk), :], v_ref[pl.ds(start, tk), :]
    q = q_ref[...]
    row = i0 + lax.broadcasted_iota(jnp.int32, (tq, tk), 0)
    col = start + lax.broadcasted_iota(jnp.int32, (tq, tk), 1)
    valid = jnp.abs(row - col) <= DIL_SIDE
    lane = lax.broadcasted_iota(jnp.int32, (tq, DIL_W), 1)
    out = jnp.zeros((tq, DIL_W), F32)
    lse = jnp.zeros((tq, DIL_W), F32)
    for hd in range(DIL_HEADS):
        in_head = (lane >= hd * DIL_HEAD_DIM) & (lane < (hd + 1) * DIL_HEAD_DIM)
        qh = jnp.where(in_head, q, jnp.zeros_like(q))
        s = lax.dot_general(qh, kw, (((1,), (1,)), ((), ())), preferred_element_type=F32)
        s = jnp.where(valid, s, NEG_INF)
        m = jnp.max(s, axis=-1, keepdims=True)
        p = jnp.exp(s - m)
        l = jnp.sum(p, axis=-1, keepdims=True)
        o = jnp.dot(p.astype(BF16), vw, preferred_element_type=F32)
        out = jnp.where(in_head, o / l, out)
        lse = jnp.where(in_head, m + jnp.log(l), lse)
    o_ref[...] = out
    lse_ref[...] = lse


def _dil_group(qd, kd, vd, g):
    B, S, W = qd.shape
    d = DIL_GROUPS[g][1]
    n = S // d
    tq = min(TQ_DIL, n)
    tk = min(n, tq + 2 * DIL_SIDE)
    view = lambda a: a.reshape(B, n, d * W)
    col = lambda r: r * N_DIL + g
    kv_spec = pl.BlockSpec((None, n, DIL_W), lambda b, r, i: (b, 0, col(r)))
    out_spec = pl.BlockSpec((None, tq, DIL_W), lambda b, r, i: (b, i, r))
    o, lse = pl.pallas_call(
        functools.partial(_dil_kernel, tk=tk),
        grid=(B, d, n // tq),
        in_specs=[pl.BlockSpec((None, tq, DIL_W), lambda b, r, i: (b, i, col(r))), kv_spec, kv_spec],
        out_specs=[out_spec, out_spec],
        out_shape=[jax.ShapeDtypeStruct((B, n, d * DIL_W), F32)] * 2,
        compiler_params=pltpu.CompilerParams(dimension_semantics=("parallel", "parallel", "arbitrary"),
                                             vmem_limit_bytes=VMEM_LIMIT),
        name=f"dil_attention_g{g}",
    )(view(qd), view(kd), view(vd))
    return o.reshape(B, S, DIL_W), lse.reshape(B, S, DIL_W)


def _merge_kernel(x_ref, ca_ref, yb_ref, om_ref, o0_ref, l0_ref, o1_ref, l1_ref, o2_ref, l2_ref,
                  g_attn_ref, w_g_ref, woa_ref, wob_ref, woc_ref, wod_ref, w_out_ref, y_ref):
    x = x_ref[...]
    h = (x * _rms_scale(x, D_MODEL) * g_attn_ref[...]).astype(BF16)
    l0, l1, l2 = l0_ref[...], l1_ref[...], l2_ref[...]
    m = jnp.maximum(jnp.maximum(l0, l1), l2)
    e0, e1, e2 = jnp.exp(l0 - m), jnp.exp(l1 - m), jnp.exp(l2 - m)
    od = ((e0 * o0_ref[...] + e1 * o1_ref[...] + e2 * o2_ref[...]) / (e0 + e1 + e2)).astype(BF16)
    branches = ((ca_ref[...], woa_ref), (yb_ref[...], wob_ref), (om_ref[...], woc_ref), (od, wod_ref))
    merged = jnp.zeros(x.shape, F32)
    for i, (act, wo_ref) in enumerate(branches):
        gate = _sigmoid(jnp.dot(h, w_g_ref[:, i * D_MODEL:(i + 1) * D_MODEL], preferred_element_type=F32))
        merged = merged + gate * jnp.dot(act, wo_ref[...], preferred_element_type=F32)
    y_ref[...] = x + jnp.dot(merged.astype(BF16), w_out_ref[...], preferred_element_type=F32)


def _merge(x, ca, yb, om, dil, lw):
    B, S, _ = x.shape
    tm = TM_MERGE
    tok_spec = lambda w: pl.BlockSpec((None, tm, w), lambda b, i: (b, i, 0))
    acts = (x, ca, yb, om) + tuple(a for pair in dil for a in pair)
    weights = (lw['g_attn'], lw['w_g'], lw['wo_a'], lw['wo_b'], lw['wo_c'], lw['wo_d'], lw['w_out'])
    return pl.pallas_call(
        _merge_kernel,
        grid=(B, S // tm),
        in_specs=[tok_spec(a.shape[-1]) for a in acts] + [_const_spec(w.shape) for w in weights],
        out_specs=tok_spec(D_MODEL),
        out_shape=jax.ShapeDtypeStruct((B, S, D_MODEL), F32),
        compiler_params=pltpu.CompilerParams(dimension_semantics=("parallel", "parallel"),
                                             vmem_limit_bytes=VMEM_LIMIT),
        name="merge",
    )(*acts, *weights)


def _ffn_kernel(prev_ref, cur_ref, next_ref, g_ref, wa_ref, wb_ref, cwa_ref, cwb_ref, cba_ref, cbb_ref, wd_ref,
                y_ref, h_ref, ua_ref, ub_ref, acc_ref):
    tm = cur_ref.shape[0]
    i = pl.program_id(1)
    last = pl.num_programs(1) - 1

    def normed(x):
        return (x * _rms_scale(x, D_MODEL) * g_ref[...]).astype(BF16)

    h_ref[0:FFN_HALO, :] = jnp.where(i > 0, normed(prev_ref[...]), jnp.zeros((FFN_HALO, D_MODEL), BF16))
    h_ref[FFN_HALO:FFN_HALO + tm, :] = normed(cur_ref[...])
    h_ref[FFN_HALO + tm:, :] = jnp.where(i < last, normed(next_ref[...]), jnp.zeros((FFN_HALO, D_MODEL), BF16))
    acc_ref[...] = jnp.zeros(acc_ref.shape, F32)

    def conv3(u_ref, cw, cb):
        return (u_ref[FFN_HALO - 1:FFN_HALO - 1 + tm, :] * cw[0:1, :] + u_ref[FFN_HALO:FFN_HALO + tm, :] * cw[1:2, :]
                + u_ref[FFN_HALO + 1:FFN_HALO + 1 + tm, :] * cw[2:3, :] + cb)

    def body(j, carry):
        he = h_ref[...]
        ua_ref[...] = jnp.dot(he, wa_ref[j], preferred_element_type=F32)
        ub_ref[...] = jnp.dot(he, wb_ref[j], preferred_element_type=F32)
        a = conv3(ua_ref, cwa_ref[j], cba_ref[j])
        b = conv3(ub_ref, cwb_ref[j], cbb_ref[j])
        act = (a * _sigmoid(a) * b).astype(BF16)
        acc_ref[...] += jnp.dot(act, wd_ref[j], preferred_element_type=F32)
        return carry

    lax.fori_loop(0, wa_ref.shape[0], body, 0)
    y_ref[...] = cur_ref[...] + acc_ref[...]


def _ffn(x, lw):
    B, S, _ = x.shape
    tm = TM_FFN
    nh = tm // FFN_HALO
    n_halo_blocks = S // FFN_HALO
    weights = (lw['g_ffn'], lw['w_up_a'], lw['w_up_b'], lw['cw_a'], lw['cw_b'], lw['cb_a'], lw['cb_b'], lw['w_down'])
    return pl.pallas_call(
        _ffn_kernel,
        grid=(B, S // tm),
        in_specs=[pl.BlockSpec((None, FFN_HALO, D_MODEL), lambda b, i: (b, jnp.maximum(i * nh - 1, 0), 0)),
                  pl.BlockSpec((None, tm, D_MODEL), lambda b, i: (b, i, 0)),
                  pl.BlockSpec((None, FFN_HALO, D_MODEL),
                               lambda b, i: (b, jnp.minimum((i + 1) * nh, n_halo_blocks - 1), 0))]
                 + [_const_spec(w.shape) for w in weights],
        out_specs=pl.BlockSpec((None, tm, D_MODEL), lambda b, i: (b, i, 0)),
        out_shape=jax.ShapeDtypeStruct((B, S, D_MODEL), F32),
        scratch_shapes=[pltpu.VMEM((tm + 2 * FFN_HALO, D_MODEL), BF16),
                        pltpu.VMEM((tm + 2 * FFN_HALO, FFN_CHUNK), F32),
                        pltpu.VMEM((tm + 2 * FFN_HALO, FFN_CHUNK), F32),
                        pltpu.VMEM((tm, D_MODEL), F32)],
        compiler_params=pltpu.CompilerParams(dimension_semantics=("parallel", "parallel"),
                                             vmem_limit_bytes=VMEM_LIMIT),
        name="conv_ffn",
    )(x, x, x, *weights)


def _rope_tables(seq, dims, theta, first_lane, period):
    half = dims // 2
    inv = jnp.exp(-math.log(theta) * jnp.arange(0, dims, 2, dtype=F32) / dims)
    ang = jnp.arange(seq, dtype=F32)[:, None] * inv[None, :]
    cos, sin = jnp.cos(ang), jnp.sin(ang)
    ones = jnp.ones((seq, period), F32)
    zeros = jnp.zeros((seq, period), F32)
    c = ones.at[:, first_lane:first_lane + dims].set(jnp.concatenate([cos, cos], axis=1))
    sa = zeros.at[:, first_lane:first_lane + half].set(-sin)
    sb = zeros.at[:, first_lane + half:first_lane + dims].set(sin)
    rep = LANES // period
    return tuple(jnp.tile(t, (1, rep)) for t in (c, sa, sb))


def _pad_heads(w, n_heads, width):
    k = w.shape[0]
    w = w.reshape(k, n_heads, width)
    return jnp.pad(w, ((0, 0), (0, 0), (0, LANES - width))).reshape(k, n_heads * LANES)


def _prep_layer(p, l):
    row = lambda a: a.reshape(1, -1).astype(F32)
    w_in = p['w_in'][l]
    o_c = N_A + N_B
    o_d = o_c + N_C
    o_g = o_d + N_D
    w_c = w_in[:, o_c:o_d]
    kr_at = MLA_NOPE
    w_kr = jnp.pad(w_c[:, MLA_Q_RANK + MLA_KV_RANK:], ((0, 0), (kr_at, LANES - kr_at - MLA_ROPE)))
    w_ukv = p['mla_w_ukv'][l].reshape(MLA_KV_RANK, MLA_HEADS, MLA_NOPE + MLA_V)
    pad_gain = lambda g: jnp.pad(g, (0, LANES - MLA_DQK)).reshape(1, LANES)
    n_chunks = D_FF // FFN_CHUNK
    w_up = p['ffn_w_up'][l]
    chunk_cols = lambda w: w.reshape(w.shape[0], n_chunks, FFN_CHUNK).transpose(1, 0, 2)
    cw, cb = p['ffn_conv_w'][l], p['ffn_conv_b'][l].reshape(1, -1)
    return {
        'g_attn': row(p['attn_norm'][l]),
        'w_ab': w_in[:, :o_c].astype(BF16),
        'w_c': jnp.concatenate([w_c[:, :MLA_Q_RANK + MLA_KV_RANK], w_kr], axis=1).astype(BF16),
        'w_d': w_in[:, o_d:o_g].astype(BF16),
        'w_g': w_in[:, o_g:].astype(BF16),
        'conv_w': p['conv_w'][l], 'conv_b': row(p['conv_b'][l]),
        'conv_g': row(p['conv_ln_g'][l]), 'conv_beta': row(p['conv_ln_b'][l]),
        'sgu_g': row(p['sgu_ln_g'][l]), 'sgu_b': row(p['sgu_ln_b'][l]),
        'ws': p['sgu_w_s'][l].astype(BF16),
        'bs': jnp.repeat(p['sgu_b_s'][l].T, SGU_CH // SGU_GROUPS, axis=1),
        'g_cq': row(p['mla_g_cq'][l]), 'g_ckv': row(p['mla_g_ckv'][l]),
        'w_uq': _pad_heads(p['mla_w_uq'][l], MLA_HEADS, MLA_DQK).astype(BF16),
        'w_uk': _pad_heads(w_ukv[:, :, :MLA_NOPE].reshape(MLA_KV_RANK, -1), MLA_HEADS, MLA_NOPE).astype(BF16),
        'w_uv': _pad_heads(w_ukv[:, :, MLA_NOPE:].reshape(MLA_KV_RANK, -1), MLA_HEADS, MLA_V).astype(BF16),
        'gq_mla': pad_gain(p['mla_g_qn'][l] * (MLA_DQK ** -0.5)), 'gk_mla': pad_gain(p['mla_g_kn'][l]),
        'gq_dil': jnp.tile(p['dil_g_qn'][l] * (DIL_HEAD_DIM ** -0.5), 2).reshape(1, LANES),
        'gk_dil': jnp.tile(p['dil_g_kn'][l], 2).reshape(1, LANES),
        'wo_a': p['conv_w_o'][l].astype(BF16), 'wo_b': p['sgu_w_o'][l].astype(BF16),
        'wo_c': _pad_heads(p['mla_w_o'][l].T, MLA_HEADS, MLA_V).T.astype(BF16),
        'wo_d': p['dil_w_o'][l].astype(BF16),
        'w_out': p['w_out'][l].astype(BF16),
        'g_ffn': row(p['ffn_norm'][l]),
        'w_up_a': chunk_cols(w_up[:, :D_FF]).astype(BF16), 'w_up_b': chunk_cols(w_up[:, D_FF:]).astype(BF16),
        'cw_a': chunk_cols(cw[:, :D_FF]), 'cw_b': chunk_cols(cw[:, D_FF:]),
        'cb_a': chunk_cols(cb[:, :D_FF]), 'cb_b': chunk_cols(cb[:, D_FF:]),
        'w_down': p['ffn_w_down'][l].reshape(n_chunks, FFN_CHUNK, D_MODEL).astype(BF16),
    }


def _trunk(x, layers):
    S = x.shape[1]
    tabs = (_rope_tables(S, MLA_ROPE, MLA_THETA, MLA_NOPE, LANES)
            + _rope_tables(S, ROPE_DIMS, ROPE_THETA, 0, DIL_HEAD_DIM))
    for lw in layers:
        ha, yb, qm, km, vm, qd, kd, vd = _inproj(x, tabs, lw)
        ca = _conv_module(ha, lw)
        om = _mla_attention(qm, km, vm)
        dil = [_dil_group(qd, kd, vd, g) for g in range(N_DIL)]
        x = _merge(x, ca, yb, om, dil, lw)
        x = _ffn(x, lw)
    return x


def kernel(x_prompt, x_sample, attn_norm, w_in, conv_w, conv_b, conv_ln_g, conv_ln_b, conv_w_o, sgu_ln_g, sgu_ln_b, sgu_w_s, sgu_b_s, sgu_w_o, mla_g_cq, mla_g_ckv, mla_w_uq, mla_w_ukv, mla_g_qn, mla_g_kn, mla_w_o, dil_g_qn, dil_g_kn, dil_w_o, w_out, ffn_norm, ffn_w_up, ffn_conv_w, ffn_conv_b, ffn_w_down):
    p = dict(attn_norm=attn_norm, w_in=w_in, conv_w=conv_w, conv_b=conv_b, conv_ln_g=conv_ln_g, conv_ln_b=conv_ln_b,
             conv_w_o=conv_w_o, sgu_ln_g=sgu_ln_g, sgu_ln_b=sgu_ln_b, sgu_w_s=sgu_w_s, sgu_b_s=sgu_b_s,
             sgu_w_o=sgu_w_o, mla_g_cq=mla_g_cq, mla_g_ckv=mla_g_ckv, mla_w_uq=mla_w_uq, mla_w_ukv=mla_w_ukv,
             mla_g_qn=mla_g_qn, mla_g_kn=mla_g_kn, mla_w_o=mla_w_o, dil_g_qn=dil_g_qn, dil_g_kn=dil_g_kn,
             dil_w_o=dil_w_o, w_out=w_out, ffn_norm=ffn_norm, ffn_w_up=ffn_w_up, ffn_conv_w=ffn_conv_w,
             ffn_conv_b=ffn_conv_b, ffn_w_down=ffn_w_down)
    layers = [_prep_layer(p, l) for l in range(DEPTH)]
    return (_trunk(x_prompt, layers), _trunk(x_sample, layers))
```

```python
import functools
import math

import jax
import jax.numpy as jnp
from jax import lax
from jax.experimental import pallas as pl
from jax.experimental.pallas import tpu as pltpu

F32 = jnp.float32
BF16 = jnp.bfloat16

D_MODEL = 1024
DEPTH = 2
EPS = 1e-6
NEG_INF = -1e30
CONV_CH = 256
CONV_WIDTH = 31
SGU_CH = 256
SGU_GROUPS = 4
SGU_CHUNK = 128
MLA_HEADS = 4
MLA_Q_RANK = 256
MLA_KV_RANK = 128
MLA_NOPE = 64
MLA_ROPE = 32
MLA_V = 64
MLA_THETA = 10000.0
DIL_GROUPS = ((128, 1), (512, 4), (2048, 16))
N_DIL = 3
DIL_HEADS = 4
DIL_HEAD_DIM = 64
ROPE_THETA = 500000.0
ROPE_DIMS = DIL_HEAD_DIM // 4
N_BRANCH = 4
D_FF = 2816
N_A = 2 * CONV_CH
N_B = 2 * SGU_CH
N_C = MLA_Q_RANK + MLA_KV_RANK + MLA_ROPE
N_D = 3 * N_DIL * DIL_HEADS * DIL_HEAD_DIM
DIL_SIDE = 64
MLA_DQK = MLA_NOPE + MLA_ROPE
DIL_W = DIL_HEADS * DIL_HEAD_DIM

LANES = 128
SUBLANES = 8
VMEM_LIMIT = 56 * 1024 * 1024

TM_PROJ = 256
TM_CONV = 512
CONV_HALO = 16
TQ_MLA = 256
TK_MLA = 512
TQ_DIL = 256
TM_MERGE = 256
TM_FFN = 512
FFN_HALO = 16
FFN_CHUNK = 256


def _const_spec(shape):
    nd = len(shape)
    return pl.BlockSpec(shape, lambda *_: (0,) * nd, pipeline_mode=pl.Buffered(1))


def _sigmoid(x):
    return 1.0 / (1.0 + jnp.exp(-x))


def _rms_scale(x, n):
    return lax.rsqrt(jnp.sum(x * x, axis=-1, keepdims=True) * (1.0 / n) + EPS)


def _layer_norm(x, g, b):
    mu = jnp.mean(x, axis=-1, keepdims=True)
    xc = x - mu
    var = jnp.mean(xc * xc, axis=-1, keepdims=True)
    return xc * lax.rsqrt(var + EPS) * g + b


def _rope(x, cos, sa, sb, half):
    return x * cos + pltpu.roll(x, LANES - half, 1) * sa + pltpu.roll(x, half, 1) * sb


def _inproj_kernel(x_ref, mcos_ref, msa_ref, msb_ref, dcos_ref, dsa_ref, dsb_ref,
                   g_attn_ref, w_ab_ref, w_c_ref, w_d_ref,
                   sgu_g_ref, sgu_b_ref, ws_ref, bs_ref,
                   g_cq_ref, g_ckv_ref, w_uq_ref, w_uk_ref, w_uv_ref, gq_mla_ref, gk_mla_ref,
                   gq_dil_ref, gk_dil_ref,
                   ha_ref, yb_ref, qm_ref, km_ref, vm_ref, *dil_and_scratch):
    dil_refs, slab_ref = dil_and_scratch[:3 * N_DIL], dil_and_scratch[3 * N_DIL]
    tm = x_ref.shape[0]
    x = x_ref[...]
    h = (x * _rms_scale(x, D_MODEL) * g_attn_ref[...]).astype(BF16)

    z_ab = jnp.dot(h, w_ab_ref[...], preferred_element_type=F32)
    ha_ref[...] = z_ab[:, :CONV_CH] * _sigmoid(z_ab[:, CONV_CH:N_A])
    zb = z_ab[:, N_A:]
    zb = 0.5 * zb * (1.0 + jnp.tanh(math.sqrt(2.0 / math.pi) * (zb + 0.044715 * (zb * zb * zb))))
    u = zb[:, :SGU_CH]
    v = _layer_norm(zb[:, SGU_CH:], sgu_g_ref[...], sgu_b_ref[...]).astype(BF16)
    lane = lax.broadcasted_iota(jnp.int32, (SGU_CHUNK, SGU_CH), 1)
    gw = SGU_CH // SGU_GROUPS
    for c in range(tm // SGU_CHUNK):
        rows = slice(c * SGU_CHUNK, (c + 1) * SGU_CHUNK)
        vc = v[rows]
        sv = jnp.dot(ws_ref[SGU_GROUPS - 1], vc, preferred_element_type=F32)
        for g in range(SGU_GROUPS - 2, -1, -1):
            sv = jnp.where(lane < (g + 1) * gw, jnp.dot(ws_ref[g], vc, preferred_element_type=F32), sv)
        yb_ref[rows, :] = (u[rows] * (sv + bs_ref[...])).astype(BF16)

    z_c = jnp.dot(h, w_c_ref[...], preferred_element_type=F32)
    c_q = z_c[:, :MLA_Q_RANK]
    c_q = (c_q * _rms_scale(c_q, MLA_Q_RANK) * g_cq_ref[...]).astype(BF16)
    c_kv = z_c[:, MLA_Q_RANK:MLA_Q_RANK + MLA_KV_RANK]
    c_kv = (c_kv * _rms_scale(c_kv, MLA_KV_RANK) * g_ckv_ref[...]).astype(BF16)
    k_rope = z_c[:, MLA_Q_RANK + MLA_KV_RANK:]
    q = jnp.dot(c_q, w_uq_ref[...], preferred_element_type=F32)
    kn = jnp.dot(c_kv, w_uk_ref[...], preferred_element_type=F32)
    vm_ref[...] = jnp.dot(c_kv, w_uv_ref[...], preferred_element_type=F32).astype(BF16)
    mcos, msa, msb = mcos_ref[...], msa_ref[...], msb_ref[...]
    for hd in range(MLA_HEADS):
        cols = slice(hd * LANES, (hd + 1) * LANES)
        qh = q[:, cols]
        qh = qh * _rms_scale(qh, MLA_DQK) * gq_mla_ref[...]
        qm_ref[:, cols] = _rope(qh, mcos, msa, msb, MLA_ROPE // 2).astype(BF16)
        kh = kn[:, cols] + k_rope
        kh = kh * _rms_scale(kh, MLA_DQK) * gk_mla_ref[...]
        km_ref[:, cols] = _rope(kh, mcos, msa, msb, MLA_ROPE // 2).astype(BF16)

    z_d = jnp.dot(h, w_d_ref[...], preferred_element_type=F32)
    n_qk = N_DIL * DIL_W
    dcos, dsa, dsb = dcos_ref[...], dsa_ref[...], dsb_ref[...]
    low = lax.broadcasted_iota(jnp.int32, (tm, LANES), 1) < DIL_HEAD_DIM
    cols_per_group = DIL_W // LANES
    for which, gain_ref in enumerate((gq_dil_ref, gk_dil_ref, None)):
        for c in range(n_qk // LANES):
            xc = z_d[:, which * n_qk + c * LANES: which * n_qk + (c + 1) * LANES]
            if gain_ref is not None:
                sq = xc * xc
                s_lo = jnp.sum(jnp.where(low, sq, 0.0), axis=-1, keepdims=True)
                s_hi = jnp.sum(jnp.where(low, 0.0, sq), axis=-1, keepdims=True)
                ms = jnp.where(low, s_lo, s_hi) * (1.0 / DIL_HEAD_DIM)
                xc = _rope(xc * lax.rsqrt(ms + EPS) * gain_ref[...], dcos, dsa, dsb, ROPE_DIMS // 2)
            g, part = divmod(c, cols_per_group)
            out_ref = dil_refs[which * N_DIL + g]
            lanes = slice(part * LANES, (part + 1) * LANES)
            d = out_ref.shape[0]
            if d == 1:
                out_ref[0, :, lanes] = xc.astype(BF16)
            else:
                slab = slab_ref.at[which * (n_qk // LANES) + c]
                slab[...] = xc
                for r in range(d):
                    out_ref[r, :, lanes] = slab[pl.ds(r, tm // d, stride=d), :].astype(BF16)


def _inproj(x, tabs, lw):
    B, S, _ = x.shape
    tm = TM_PROJ
    grid = (B, S // tm)
    tok = lambda w, dt: jax.ShapeDtypeStruct((B, S, w), dt)
    tok_spec = lambda w: pl.BlockSpec((None, tm, w), lambda b, i: (b, i, 0))
    tab_spec = pl.BlockSpec((tm, LANES), lambda b, i: (i, 0))
    weights = (lw['g_attn'], lw['w_ab'], lw['w_c'], lw['w_d'], lw['sgu_g'], lw['sgu_b'], lw['ws'], lw['bs'],
               lw['g_cq'], lw['g_ckv'], lw['w_uq'], lw['w_uk'], lw['w_uv'], lw['gq_mla'], lw['gk_mla'],
               lw['gq_dil'], lw['gk_dil'])
    dils = [d for _ in range(3) for (_, d) in DIL_GROUPS]
    outs = pl.pallas_call(
        _inproj_kernel,
        grid=grid,
        in_specs=[tok_spec(D_MODEL)] + [tab_spec] * 6 + [_const_spec(w.shape) for w in weights],
        out_specs=[tok_spec(CONV_CH), tok_spec(SGU_CH), tok_spec(4 * LANES), tok_spec(4 * LANES), tok_spec(4 * LANES)]
                  + [pl.BlockSpec((None, d, tm // d, DIL_W), lambda b, i: (b, 0, i, 0)) for d in dils],
        out_shape=[tok(CONV_CH, F32), tok(SGU_CH, BF16), tok(4 * LANES, BF16), tok(4 * LANES, BF16), tok(4 * LANES, BF16)]
                  + [jax.ShapeDtypeStruct((B, d, S // d, DIL_W), BF16) for d in dils],
        scratch_shapes=[pltpu.VMEM((3 * N_DIL * DIL_W // LANES, tm, LANES), F32)],
        compiler_params=pltpu.CompilerParams(dimension_semantics=("parallel", "parallel"),
                                             vmem_limit_bytes=VMEM_LIMIT),
        name="inproj",
    )(x, *tabs, *weights)
    return outs[:5], outs[5:5 + N_DIL], outs[5 + N_DIL:5 + 2 * N_DIL], outs[5 + 2 * N_DIL:]


def _conv_kernel(prev_ref, cur_ref, next_ref, w_ref, b_ref, g_ref, beta_ref, o_ref, ext_ref):
    tm = cur_ref.shape[0]
    i = pl.program_id(1)
    last = pl.num_programs(1) - 1
    ext_ref[0:CONV_HALO, :] = jnp.where(i > 0, prev_ref[...], 0.0)
    ext_ref[CONV_HALO:CONV_HALO + tm, :] = cur_ref[...]
    ext_ref[CONV_HALO + tm:, :] = jnp.where(i < last, next_ref[...], 0.0)
    acc = jnp.zeros((tm, CONV_CH), F32) + b_ref[...]
    base = CONV_HALO - CONV_WIDTH // 2
    for k in range(CONV_WIDTH):
        acc = acc + ext_ref[base + k:base + k + tm, :] * w_ref[k:k + 1, :]
    y = _layer_norm(acc, g_ref[...], beta_ref[...])
    o_ref[...] = (y * _sigmoid(y)).astype(BF16)


def _conv_module(ha, lw):
    B, S, _ = ha.shape
    tm = TM_CONV
    nh = tm // CONV_HALO
    n_halo_blocks = S // CONV_HALO
    weights = (lw['conv_w'], lw['conv_b'], lw['conv_g'], lw['conv_beta'])
    return pl.pallas_call(
        _conv_kernel,
        grid=(B, S // tm),
        in_specs=[pl.BlockSpec((None, CONV_HALO, CONV_CH), lambda b, i: (b, jnp.maximum(i * nh - 1, 0), 0)),
                  pl.BlockSpec((None, tm, CONV_CH), lambda b, i: (b, i, 0)),
                  pl.BlockSpec((None, CONV_HALO, CONV_CH),
                               lambda b, i: (b, jnp.minimum((i + 1) * nh, n_halo_blocks - 1), 0))]
                 + [_const_spec(w.shape) for w in weights],
        out_specs=pl.BlockSpec((None, tm, CONV_CH), lambda b, i: (b, i, 0)),
        out_shape=jax.ShapeDtypeStruct((B, S, CONV_CH), BF16),
        scratch_shapes=[pltpu.VMEM((tm + 2 * CONV_HALO, CONV_CH), F32)],
        compiler_params=pltpu.CompilerParams(dimension_semantics=("parallel", "parallel"),
                                             vmem_limit_bytes=VMEM_LIMIT),
        name="conv_module",
    )(ha, ha, ha, *weights)


def _mla_kernel(q_ref, k_ref, v_ref, o_ref):
    for hd in range(MLA_HEADS):
        cols = slice(hd * LANES, (hd + 1) * LANES)
        s = lax.dot_general(q_ref[:, cols], k_ref[:, cols], (((1,), (1,)), ((), ())), preferred_element_type=F32)
        p = jnp.exp(s - jnp.max(s, axis=-1, keepdims=True))
        l = jnp.sum(p, axis=-1, keepdims=True)
        o = jnp.dot(p.astype(BF16), v_ref[:, cols], preferred_element_type=F32)
        o_ref[:, cols] = (o / l).astype(BF16)


def _mla_attention(q, k, v):
    B, S, W = q.shape
    tq = TQ_MLA
    return pl.pallas_call(
        _mla_kernel,
        grid=(B, S // tq),
        in_specs=[pl.BlockSpec((None, tq, W), lambda b, i: (b, i, 0)),
                  pl.BlockSpec((None, S, W), lambda b, i: (b, 0, 0)),
                  pl.BlockSpec((None, S, W), lambda b, i: (b, 0, 0))],
        out_specs=pl.BlockSpec((None, tq, W), lambda b, i: (b, i, 0)),
        out_shape=jax.ShapeDtypeStruct((B, S, W), BF16),
        compiler_params=pltpu.CompilerParams(dimension_semantics=("parallel", "arbitrary"),
                                             vmem_limit_bytes=VMEM_LIMIT),
        name="mla_attention",
    )(q, k, v)


def _dil_kernel(q_ref, k_ref, v_ref, o_ref, lse_ref, *, tk):
    tq = q_ref.shape[0]
    n = k_ref.shape[0]
    i0 = pl.program_id(2) * tq
    if tk == n:
        start = 0
        kw, vw = k_ref[...], v_ref[...]
    else:
        start = pl.multiple_of(jnp.clip(i0 - DIL_SIDE, 0, n - tk), DIL_SIDE)
        kw, vw = k_ref[pl.ds(start, tk), :], v_ref[pl.ds(start, tk), :]
    q = q_ref[...]
    row = i0 + lax.broadcasted_iota(jnp.int32, (tq, tk), 0)
    col = start + lax.broadcasted_iota(jnp.int32, (tq, tk), 1)
    valid = jnp.abs(row - col) <= DIL_SIDE
    lane = lax.broadcasted_iota(jnp.int32, (tq, DIL_W), 1)
    out = jnp.zeros((tq, DIL_W), F32)
    lse = jnp.zeros((tq, DIL_W), F32)
    for hd in range(DIL_HEADS):
        in_head = (lane >= hd * DIL_HEAD_DIM) & (lane < (hd + 1) * DIL_HEAD_DIM)
        qh = jnp.where(in_head, q, jnp.zeros_like(q))
        s = lax.dot_general(qh, kw, (((1,), (1,)), ((), ())), preferred_element_type=F32)
        s = jnp.where(valid, s, NEG_INF)
        m = jnp.max(s, axis=-1, keepdims=True)
        p = jnp.exp(s - m)
        l = jnp.sum(p, axis=-1, keepdims=True)
        o = jnp.dot(p.astype(BF16), vw, preferred_element_type=F32)
        out = jnp.where(in_head, o / l, out)
        lse = jnp.where(in_head, m + jnp.log(l), lse)
    o_ref[...] = out
    lse_ref[...] = lse


def _dil_group(q, k, v, g):
    B, d, n, _ = q.shape
    tq = min(TQ_DIL, n)
    tk = min(n, tq + 2 * DIL_SIDE)
    q_spec = pl.BlockSpec((None, None, tq, DIL_W), lambda b, r, i: (b, r, i, 0))
    kv_spec = pl.BlockSpec((None, None, n, DIL_W), lambda b, r, i: (b, r, 0, 0))
    return pl.pallas_call(
        functools.partial(_dil_kernel, tk=tk),
        grid=(B, d, n // tq),
        in_specs=[q_spec, kv_spec, kv_spec],
        out_specs=[q_spec, q_spec],
        out_shape=[jax.ShapeDtypeStruct((B, d, n, DIL_W), F32)] * 2,
        compiler_params=pltpu.CompilerParams(dimension_semantics=("parallel", "parallel", "arbitrary"),
                                             vmem_limit_bytes=VMEM_LIMIT),
        name=f"dil_attention_g{g}",
    )(q, k, v)


def _merge_kernel(x_ref, ca_ref, yb_ref, om_ref, o0_ref, l0_ref, o1_ref, l1_ref, o2_ref, l2_ref,
                  g_attn_ref, w_g_ref, woa_ref, wob_ref, woc_ref, wod_ref, w_out_ref, y_ref, slab_ref):
    x = x_ref[...]
    tm = x.shape[0]
    h = (x * _rms_scale(x, D_MODEL) * g_attn_ref[...]).astype(BF16)

    def natural(ref, base):
        d = ref.shape[0]
        if d == 1:
            return ref[0]
        parts = []
        for part in range(DIL_W // LANES):
            slab = slab_ref.at[base + part]
            for r in range(d):
                slab[pl.ds(r, tm // d, stride=d), :] = ref[r, :, part * LANES:(part + 1) * LANES]
            parts.append(slab[...])
        return jnp.concatenate(parts, axis=1)

    o0, l0 = natural(o0_ref, 0), natural(l0_ref, 2)
    o1, l1 = natural(o1_ref, 4), natural(l1_ref, 6)
    o2, l2 = natural(o2_ref, 8), natural(l2_ref, 10)
    m = jnp.maximum(jnp.maximum(l0, l1), l2)
    e0, e1, e2 = jnp.exp(l0 - m), jnp.exp(l1 - m), jnp.exp(l2 - m)
    od = ((e0 * o0 + e1 * o1 + e2 * o2) / (e0 + e1 + e2)).astype(BF16)
    branches = ((ca_ref[...], woa_ref), (yb_ref[...], wob_ref), (om_ref[...], woc_ref), (od, wod_ref))
    merged = jnp.zeros(x.shape, F32)
    for i, (act, wo_ref) in enumerate(branches):
        gate = _sigmoid(jnp.dot(h, w_g_ref[:, i * D_MODEL:(i + 1) * D_MODEL], preferred_element_type=F32))
        merged = merged + gate * jnp.dot(act, wo_ref[...], preferred_element_type=F32)
    y_ref[...] = x + jnp.dot(merged.astype(BF16), w_out_ref[...], preferred_element_type=F32)


def _merge(x, ca, yb, om, dil, lw):
    B, S, _ = x.shape
    tm = TM_MERGE
    tok_spec = lambda w: pl.BlockSpec((None, tm, w), lambda b, i: (b, i, 0))
    acts = (x, ca, yb, om)
    dil_acts = tuple(a for pair in dil for a in pair)
    dil_spec = lambda d: pl.BlockSpec((None, d, tm // d, DIL_W), lambda b, i: (b, 0, i, 0))
    weights = (lw['g_attn'], lw['w_g'], lw['wo_a'], lw['wo_b'], lw['wo_c'], lw['wo_d'], lw['w_out'])
    return pl.pallas_call(
        _merge_kernel,
        grid=(B, S // tm),
        in_specs=[tok_spec(a.shape[-1]) for a in acts] + [dil_spec(a.shape[1]) for a in dil_acts]
                 + [_const_spec(w.shape) for w in weights],
        out_specs=tok_spec(D_MODEL),
        out_shape=jax.ShapeDtypeStruct((B, S, D_MODEL), F32),
        scratch_shapes=[pltpu.VMEM((2 * N_DIL * DIL_W // LANES, tm, LANES), F32)],
        compiler_params=pltpu.CompilerParams(dimension_semantics=("parallel", "parallel"),
                                             vmem_limit_bytes=VMEM_LIMIT),
        name="merge",
    )(*acts, *dil_acts, *weights)


def _ffn_kernel(prev_ref, cur_ref, next_ref, g_ref, wa_ref, wb_ref, cwa_ref, cwb_ref, cba_ref, cbb_ref, wd_ref,
                y_ref, h_ref, ua_ref, ub_ref, act_ref):
    tm = cur_ref.shape[0]
    i = pl.program_id(1)
    last = pl.num_programs(1) - 1

    def normed(x):
        return (x * _rms_scale(x, D_MODEL) * g_ref[...]).astype(BF16)

    h_ref[0:FFN_HALO, :] = jnp.where(i > 0, normed(prev_ref[...]), jnp.zeros((FFN_HALO, D_MODEL), BF16))
    h_ref[FFN_HALO:FFN_HALO + tm, :] = normed(cur_ref[...])
    h_ref[FFN_HALO + tm:, :] = jnp.where(i < last, normed(next_ref[...]), jnp.zeros((FFN_HALO, D_MODEL), BF16))

    def conv3(u_ref, cw, cb):
        return (u_ref[FFN_HALO - 1:FFN_HALO - 1 + tm, :] * cw[0:1, :] + u_ref[FFN_HALO:FFN_HALO + tm, :] * cw[1:2, :]
                + u_ref[FFN_HALO + 1:FFN_HALO + 1 + tm, :] * cw[2:3, :] + cb)

    for j in range(wa_ref.shape[0]):
        slot = j % 2
        he = h_ref[...]
        ua_ref[slot] = jnp.dot(he, wa_ref[j], preferred_element_type=F32)
        ub_ref[slot] = jnp.dot(he, wb_ref[j], preferred_element_type=F32)
        a = conv3(ua_ref.at[slot], cwa_ref[j], cba_ref[j])
        b = conv3(ub_ref.at[slot], cwb_ref[j], cbb_ref[j])
        act_ref[:, j * FFN_CHUNK:(j + 1) * FFN_CHUNK] = (a * _sigmoid(a) * b).astype(BF16)
    y_ref[...] = cur_ref[...] + jnp.dot(act_ref[...], wd_ref[...], preferred_element_type=F32)


def _ffn(x, lw):
    B, S, _ = x.shape
    tm = TM_FFN
    nh = tm // FFN_HALO
    n_halo_blocks = S // FFN_HALO
    weights = (lw['g_ffn'], lw['w_up_a'], lw['w_up_b'], lw['cw_a'], lw['cw_b'], lw['cb_a'], lw['cb_b'], lw['w_down'])
    return pl.pallas_call(
        _ffn_kernel,
        grid=(B, S // tm),
        in_specs=[pl.BlockSpec((None, FFN_HALO, D_MODEL), lambda b, i: (b, jnp.maximum(i * nh - 1, 0), 0)),
                  pl.BlockSpec((None, tm, D_MODEL), lambda b, i: (b, i, 0)),
                  pl.BlockSpec((None, FFN_HALO, D_MODEL),
                               lambda b, i: (b, jnp.minimum((i + 1) * nh, n_halo_blocks - 1), 0))]
                 + [_const_spec(w.shape) for w in weights],
        out_specs=pl.BlockSpec((None, tm, D_MODEL), lambda b, i: (b, i, 0)),
        out_shape=jax.ShapeDtypeStruct((B, S, D_MODEL), F32),
        scratch_shapes=[pltpu.VMEM((tm + 2 * FFN_HALO, D_MODEL), BF16),
                        pltpu.VMEM((2, tm + 2 * FFN_HALO, FFN_CHUNK), F32),
                        pltpu.VMEM((2, tm + 2 * FFN_HALO, FFN_CHUNK), F32),
                        pltpu.VMEM((tm, D_FF), BF16)],
        compiler_params=pltpu.CompilerParams(dimension_semantics=("parallel", "parallel"),
                                             vmem_limit_bytes=VMEM_LIMIT),
        name="conv_ffn",
    )(x, x, x, *weights)


def _rope_tables(seq, dims, theta, first_lane, period):
    half = dims // 2
    inv = jnp.exp(-math.log(theta) * jnp.arange(0, dims, 2, dtype=F32) / dims)
    ang = jnp.arange(seq, dtype=F32)[:, None] * inv[None, :]
    cos, sin = jnp.cos(ang), jnp.sin(ang)
    ones = jnp.ones((seq, period), F32)
    zeros = jnp.zeros((seq, period), F32)
    c = ones.at[:, first_lane:first_lane + dims].set(jnp.concatenate([cos, cos], axis=1))
    sa = zeros.at[:, first_lane:first_lane + half].set(-sin)
    sb = zeros.at[:, first_lane + half:first_lane + dims].set(sin)
    rep = LANES // period
    return tuple(jnp.tile(t, (1, rep)) for t in (c, sa, sb))


def _pad_heads(w, n_heads, width):
    k = w.shape[0]
    w = w.reshape(k, n_heads, width)
    return jnp.pad(w, ((0, 0), (0, 0), (0, LANES - width))).reshape(k, n_heads * LANES)


def _prep_layer(p, l):
    row = lambda a: a.reshape(1, -1).astype(F32)
    w_in = p['w_in'][l]
    o_c = N_A + N_B
    o_d = o_c + N_C
    o_g = o_d + N_D
    w_c = w_in[:, o_c:o_d]
    kr_at = MLA_NOPE
    w_kr = jnp.pad(w_c[:, MLA_Q_RANK + MLA_KV_RANK:], ((0, 0), (kr_at, LANES - kr_at - MLA_ROPE)))
    w_ukv = p['mla_w_ukv'][l].reshape(MLA_KV_RANK, MLA_HEADS, MLA_NOPE + MLA_V)
    pad_gain = lambda g: jnp.pad(g, (0, LANES - MLA_DQK)).reshape(1, LANES)
    n_chunks = D_FF // FFN_CHUNK
    w_up = p['ffn_w_up'][l]
    chunk_cols = lambda w: w.reshape(w.shape[0], n_chunks, FFN_CHUNK).transpose(1, 0, 2)
    cw, cb = p['ffn_conv_w'][l], p['ffn_conv_b'][l].reshape(1, -1)
    return {
        'g_attn': row(p['attn_norm'][l]),
        'w_ab': w_in[:, :o_c].astype(BF16),
        'w_c': jnp.concatenate([w_c[:, :MLA_Q_RANK + MLA_KV_RANK], w_kr], axis=1).astype(BF16),
        'w_d': w_in[:, o_d:o_g].astype(BF16),
        'w_g': w_in[:, o_g:].astype(BF16),
        'conv_w': p['conv_w'][l], 'conv_b': row(p['conv_b'][l]),
        'conv_g': row(p['conv_ln_g'][l]), 'conv_beta': row(p['conv_ln_b'][l]),
        'sgu_g': row(p['sgu_ln_g'][l]), 'sgu_b': row(p['sgu_ln_b'][l]),
        'ws': p['sgu_w_s'][l].astype(BF16),
        'bs': jnp.repeat(p['sgu_b_s'][l].T, SGU_CH // SGU_GROUPS, axis=1),
        'g_cq': row(p['mla_g_cq'][l]), 'g_ckv': row(p['mla_g_ckv'][l]),
        'w_uq': _pad_heads(p['mla_w_uq'][l], MLA_HEADS, MLA_DQK).astype(BF16),
        'w_uk': _pad_heads(w_ukv[:, :, :MLA_NOPE].reshape(MLA_KV_RANK, -1), MLA_HEADS, MLA_NOPE).astype(BF16),
        'w_uv': _pad_heads(w_ukv[:, :, MLA_NOPE:].reshape(MLA_KV_RANK, -1), MLA_HEADS, MLA_V).astype(BF16),
        'gq_mla': pad_gain(p['mla_g_qn'][l] * (MLA_DQK ** -0.5)), 'gk_mla': pad_gain(p['mla_g_kn'][l]),
        'gq_dil': jnp.tile(p['dil_g_qn'][l] * (DIL_HEAD_DIM ** -0.5), 2).reshape(1, LANES),
        'gk_dil': jnp.tile(p['dil_g_kn'][l], 2).reshape(1, LANES),
        'wo_a': p['conv_w_o'][l].astype(BF16), 'wo_b': p['sgu_w_o'][l].astype(BF16),
        'wo_c': _pad_heads(p['mla_w_o'][l].T, MLA_HEADS, MLA_V).T.astype(BF16),
        'wo_d': p['dil_w_o'][l].astype(BF16),
        'w_out': p['w_out'][l].astype(BF16),
        'g_ffn': row(p['ffn_norm'][l]),
        'w_up_a': chunk_cols(w_up[:, :D_FF]).astype(BF16), 'w_up_b': chunk_cols(w_up[:, D_FF:]).astype(BF16),
        'cw_a': chunk_cols(cw[:, :D_FF]), 'cw_b': chunk_cols(cw[:, D_FF:]),
        'cb_a': chunk_cols(cb[:, :D_FF]), 'cb_b': chunk_cols(cb[:, D_FF:]),
        'w_down': p['ffn_w_down'][l].astype(BF16),
    }


def _trunk(x, layers):
    S = x.shape[1]
    tabs = (_rope_tables(S, MLA_ROPE, MLA_THETA, MLA_NOPE, LANES)
            + _rope_tables(S, ROPE_DIMS, ROPE_THETA, 0, DIL_HEAD_DIM))
    for lw in layers:
        (ha, yb, qm, km, vm), qd, kd, vd = _inproj(x, tabs, lw)
        ca = _conv_module(ha, lw)
        om = _mla_attention(qm, km, vm)
        dil = [_dil_group(qd[g], kd[g], vd[g], g) for g in range(N_DIL)]
        x = _merge(x, ca, yb, om, dil, lw)
        x = _ffn(x, lw)
    return x


def kernel(x_prompt, x_sample, attn_norm, w_in, conv_w, conv_b, conv_ln_g, conv_ln_b, conv_w_o, sgu_ln_g, sgu_ln_b, sgu_w_s, sgu_b_s, sgu_w_o, mla_g_cq, mla_g_ckv, mla_w_uq, mla_w_ukv, mla_g_qn, mla_g_kn, mla_w_o, dil_g_qn, dil_g_kn, dil_w_o, w_out, ffn_norm, ffn_w_up, ffn_conv_w, ffn_conv_b, ffn_w_down):
    p = dict(attn_norm=attn_norm, w_in=w_in, conv_w=conv_w, conv_b=conv_b, conv_ln_g=conv_ln_g, conv_ln_b=conv_ln_b,
             conv_w_o=conv_w_o, sgu_ln_g=sgu_ln_g, sgu_ln_b=sgu_ln_b, sgu_w_s=sgu_w_s, sgu_b_s=sgu_b_s,
             sgu_w_o=sgu_w_o, mla_g_cq=mla_g_cq, mla_g_ckv=mla_g_ckv, mla_w_uq=mla_w_uq, mla_w_ukv=mla_w_ukv,
             mla_g_qn=mla_g_qn, mla_g_kn=mla_g_kn, mla_w_o=mla_w_o, dil_g_qn=dil_g_qn, dil_g_kn=dil_g_kn,
             dil_w_o=dil_w_o, w_out=w_out, ffn_norm=ffn_norm, ffn_w_up=ffn_w_up, ffn_conv_w=ffn_conv_w,
             ffn_conv_b=ffn_conv_b, ffn_w_down=ffn_w_down)
    layers = [_prep_layer(p, l) for l in range(DEPTH)]
    return (_trunk(x_prompt, layers), _trunk(x_sample, layers))
```

```python
import functools
import math

import jax
import jax.numpy as jnp
import numpy as np
from jax import lax
from jax.experimental import pallas as pl
from jax.experimental.pallas import tpu as pltpu

F32 = jnp.float32
BF16 = jnp.bfloat16

D_MODEL = 1024
DEPTH = 2
EPS = 1e-6
NEG_INF = -1e30
CONV_CH = 256
CONV_WIDTH = 31
SGU_CH = 256
SGU_GROUPS = 4
SGU_CHUNK = 128
MLA_HEADS = 4
MLA_Q_RANK = 256
MLA_KV_RANK = 128
MLA_NOPE = 64
MLA_ROPE = 32
MLA_V = 64
MLA_THETA = 10000.0
DIL_GROUPS = ((128, 1), (512, 4), (2048, 16))
N_DIL = 3
DIL_HEADS = 4
DIL_HEAD_DIM = 64
ROPE_THETA = 500000.0
ROPE_DIMS = DIL_HEAD_DIM // 4
N_BRANCH = 4
D_FF = 2816
N_A = 2 * CONV_CH
N_B = 2 * SGU_CH
N_C = MLA_Q_RANK + MLA_KV_RANK + MLA_ROPE
N_D = 3 * N_DIL * DIL_HEADS * DIL_HEAD_DIM
DIL_SIDE = 64
MLA_DQK = MLA_NOPE + MLA_ROPE
DIL_W = DIL_HEADS * DIL_HEAD_DIM
LOG2E = math.log2(math.e)
LN2 = math.log(2.0)

LANES = 128
SUBLANES = 8
VMEM_LIMIT = 56 * 1024 * 1024

TM_PROJ = 512
SUB_PROJ = 256
TM_CONV = 512
CONV_HALO = 16
MLA_SCORE_ELEMS = 1 << 20
TQ_DIL = 256
DIL_ROWS_PER_STEP = 512
TM_MERGE = 256
TM_FFN = 512
FFN_HALO = 16
FFN_CHUNK = 256


def _const_spec(shape):
    nd = len(shape)
    return pl.BlockSpec(shape, lambda *_: (0,) * nd, pipeline_mode=pl.Buffered(1))


def _sigmoid(x):
    return 1.0 / (1.0 + jnp.exp(-x))


def _rms_scale(x, n):
    return lax.rsqrt(jnp.sum(x * x, axis=-1, keepdims=True) * (1.0 / n) + EPS)


def _layer_norm(x, g, b):
    mu = jnp.mean(x, axis=-1, keepdims=True)
    xc = x - mu
    var = jnp.mean(xc * xc, axis=-1, keepdims=True)
    return xc * lax.rsqrt(var + EPS) * g + b


def _rope(x, cos, sin):
    return x * cos + pltpu.roll(x, LANES // 2, 1) * sin


def _inproj_kernel(x_ref, mcos_ref, msin_ref, dcos_ref, dsin_ref,
                   g_attn_ref, w_ab_ref, w_c_ref, w_d_ref,
                   sgu_g_ref, sgu_b_ref, ws_ref, bs_ref,
                   g_cq_ref, g_ckv_ref, w_uq_ref, w_uk_ref, w_uv_ref, gq_mla_ref, gk_mla_ref,
                   gq_dil_ref, gk_dil_ref,
                   ha_ref, yb_ref, qm_ref, km_ref, vm_ref, *dil_and_scratch):
    dil_refs, slab_ref = dil_and_scratch[:3 * N_DIL], dil_and_scratch[3 * N_DIL]
    tm = x_ref.shape[0]
    x = x_ref[...]
    h = (x * _rms_scale(x, D_MODEL) * g_attn_ref[...]).astype(BF16)

    z_d = jnp.dot(h, w_d_ref[...], preferred_element_type=F32)
    z_c = jnp.dot(h, w_c_ref[...], preferred_element_type=F32)
    z_ab = jnp.dot(h, w_ab_ref[...], preferred_element_type=F32)

    dcos, dsin = dcos_ref[...], dsin_ref[...]
    lane_head = (lax.broadcasted_iota(jnp.int32, (tm, LANES), 1) % 64) // 16
    head_masks = [lane_head == hd for hd in range(DIL_HEADS)]
    slab_i = 0
    for which, gain_ref in enumerate((gq_dil_ref, gk_dil_ref, None)):
        for g in range(N_DIL):
            base = (which * N_DIL + g) * DIL_W
            x0, x1 = z_d[:, base:base + LANES], z_d[:, base + LANES:base + DIL_W]
            if gain_ref is not None:
                sq = x0 * x0 + x1 * x1
                ms = jnp.zeros((tm, LANES), F32)
                for hd in range(DIL_HEADS):
                    ms = jnp.where(head_masks[hd],
                                   jnp.sum(jnp.where(head_masks[hd], sq, 0.0), axis=-1, keepdims=True), ms)
                scale = lax.rsqrt(ms * (1.0 / DIL_HEAD_DIM) + EPS)
                x0 = _rope(x0 * scale * gain_ref[:, :LANES], dcos, dsin)
                x1 = x1 * scale * gain_ref[:, LANES:]
            out_ref = dil_refs[which * N_DIL + g]
            d = out_ref.shape[0]
            for part, xc in enumerate((x0, x1)):
                lanes = slice(part * LANES, (part + 1) * LANES)
                if d == 1:
                    out_ref[0, :, lanes] = xc.astype(BF16)
                else:
                    slab = slab_ref.at[slab_i]
                    slab_i += 1
                    slab[...] = xc
                    for r in range(d):
                        out_ref[r, :, lanes] = slab[pl.ds(r, tm // d, stride=d), :].astype(BF16)

    c_q = z_c[:, :MLA_Q_RANK]
    c_q = (c_q * _rms_scale(c_q, MLA_Q_RANK) * g_cq_ref[...]).astype(BF16)
    c_kv = z_c[:, MLA_Q_RANK:MLA_Q_RANK + MLA_KV_RANK]
    c_kv = (c_kv * _rms_scale(c_kv, MLA_KV_RANK) * g_ckv_ref[...]).astype(BF16)
    k_rope = z_c[:, MLA_Q_RANK + MLA_KV_RANK:]
    q = jnp.dot(c_q, w_uq_ref[...], preferred_element_type=F32)
    kn = jnp.dot(c_kv, w_uk_ref[...], preferred_element_type=F32)
    v_lane = lax.broadcasted_iota(jnp.int32, (tm, MLA_HEADS * LANES), 1) % LANES
    vm_ref[...] = jnp.where(v_lane == MLA_V, 1.0,
                            jnp.dot(c_kv, w_uv_ref[...], preferred_element_type=F32)).astype(BF16)
    mcos, msin = mcos_ref[...], msin_ref[...]
    for hd in range(MLA_HEADS):
        cols = slice(hd * LANES, (hd + 1) * LANES)
        qh = q[:, cols]
        qh = qh * _rms_scale(qh, MLA_DQK) * gq_mla_ref[...]
        qm_ref[:, cols] = _rope(qh, mcos, msin).astype(BF16)
        kh = kn[:, cols] + k_rope
        kh = kh * _rms_scale(kh, MLA_DQK) * gk_mla_ref[...]
        km_ref[:, cols] = _rope(kh, mcos, msin).astype(BF16)

    ha_ref[...] = z_ab[:, :CONV_CH] * _sigmoid(z_ab[:, CONV_CH:N_A])
    zb = z_ab[:, N_A:]
    zb = 0.5 * zb * (1.0 + jnp.tanh(math.sqrt(2.0 / math.pi) * (zb + 0.044715 * (zb * zb * zb))))
    u = zb[:, :SGU_CH]
    v = _layer_norm(zb[:, SGU_CH:], sgu_g_ref[...], sgu_b_ref[...]).astype(BF16)
    lane = lax.broadcasted_iota(jnp.int32, (SGU_CHUNK, SGU_CH), 1)
    gw = SGU_CH // SGU_GROUPS
    for c in range(tm // SGU_CHUNK):
        rows = slice(c * SGU_CHUNK, (c + 1) * SGU_CHUNK)
        vc = v[rows]
        sv = jnp.dot(ws_ref[SGU_GROUPS - 1], vc, preferred_element_type=F32)
        for g in range(SGU_GROUPS - 2, -1, -1):
            sv = jnp.where(lane < (g + 1) * gw, jnp.dot(ws_ref[g], vc, preferred_element_type=F32), sv)
        yb_ref[rows, :] = (u[rows] * (sv + bs_ref[...])).astype(BF16)


N_PROJ_TOK_OUT = 5
N_PROJ_SLABS = 3 * N_DIL * DIL_W // LANES


def _inproj_tile_kernel(*refs, n_rowwise_in, n_weights):
    rowwise_in = refs[:n_rowwise_in]
    weights = refs[n_rowwise_in:n_rowwise_in + n_weights]
    outs = refs[n_rowwise_in + n_weights:]
    tok_outs, dil_outs, slab_ref = outs[:N_PROJ_TOK_OUT], outs[N_PROJ_TOK_OUT:-1], outs[-1]
    for sub in range(rowwise_in[0].shape[0] // SUB_PROJ):
        rows = pl.ds(sub * SUB_PROJ, SUB_PROJ)
        dil_views = []
        for r in dil_outs:
            per_class = SUB_PROJ // r.shape[0]
            dil_views.append(r.at[:, pl.ds(sub * per_class, per_class)])
        _inproj_kernel(*[r.at[rows] for r in rowwise_in], *weights, *[r.at[rows] for r in tok_outs], *dil_views,
                       slab_ref.at[pl.ds(sub * N_PROJ_SLABS, N_PROJ_SLABS)])


def _inproj(x, tabs, lw):
    B, S, _ = x.shape
    tm = TM_PROJ
    grid = (B, S // tm)
    tok = lambda w, dt: jax.ShapeDtypeStruct((B, S, w), dt)
    tok_spec = lambda w: pl.BlockSpec((None, tm, w), lambda b, i: (b, i, 0))
    tab_spec = pl.BlockSpec((tm, LANES), lambda b, i: (i, 0))
    weights = (lw['g_attn'], lw['w_ab'], lw['w_c'], lw['w_d'], lw['sgu_g'], lw['sgu_b'], lw['ws'], lw['bs'],
               lw['g_cq'], lw['g_ckv'], lw['w_uq'], lw['w_uk'], lw['w_uv'], lw['gq_mla'], lw['gk_mla'],
               lw['gq_dil'], lw['gk_dil'])
    dils = [d for _ in range(3) for (_, d) in DIL_GROUPS]
    outs = pl.pallas_call(
        functools.partial(_inproj_tile_kernel, n_rowwise_in=1 + len(tabs), n_weights=len(weights)),
        grid=grid,
        in_specs=[tok_spec(D_MODEL)] + [tab_spec] * len(tabs) + [_const_spec(w.shape) for w in weights],
        out_specs=[tok_spec(CONV_CH), tok_spec(SGU_CH), tok_spec(4 * LANES), tok_spec(4 * LANES), tok_spec(4 * LANES)]
                  + [pl.BlockSpec((None, d, tm // d, DIL_W), lambda b, i: (b, 0, i, 0)) for d in dils],
        out_shape=[tok(CONV_CH, F32), tok(SGU_CH, BF16), tok(4 * LANES, BF16), tok(4 * LANES, BF16), tok(4 * LANES, BF16)]
                  + [jax.ShapeDtypeStruct((B, d, S // d, DIL_W), BF16) for d in dils],
        scratch_shapes=[pltpu.VMEM((tm // SUB_PROJ * N_PROJ_SLABS, SUB_PROJ, LANES), F32)],
        compiler_params=pltpu.CompilerParams(dimension_semantics=("parallel", "parallel"),
                                             vmem_limit_bytes=VMEM_LIMIT),
        name="inproj",
    )(x, *tabs, *weights)
    return outs[:5], outs[5:5 + N_DIL], outs[5 + N_DIL:5 + 2 * N_DIL], outs[5 + 2 * N_DIL:]


def _conv_kernel(prev_ref, cur_ref, next_ref, w_ref, b_ref, g_ref, beta_ref, o_ref, ext_ref, shift_ref):
    tm = cur_ref.shape[0]
    i = pl.program_id(1)
    last = pl.num_programs(1) - 1
    ext_ref[0:CONV_HALO, :] = jnp.where(i > 0, prev_ref[...], 0.0)
    ext_ref[CONV_HALO:CONV_HALO + tm, :] = cur_ref[...]
    ext_ref[CONV_HALO + tm:, :] = jnp.where(i < last, next_ref[...], 0.0)
    n_shift = shift_ref.shape[1]
    for s in range(1, SUBLANES):
        shift_ref[s] = ext_ref[s:s + n_shift, :]
    acc = jnp.zeros((tm, CONV_CH), F32) + b_ref[...]
    base = CONV_HALO - CONV_WIDTH // 2
    for k in range(CONV_WIDTH):
        a, s = divmod(base + k, SUBLANES)
        src = ext_ref if s == 0 else shift_ref.at[s]
        acc = acc + src[a * SUBLANES:a * SUBLANES + tm, :] * w_ref[k:k + 1, :]
    y = _layer_norm(acc, g_ref[...], beta_ref[...])
    o_ref[...] = (y * _sigmoid(y)).astype(BF16)


def _conv_module(ha, lw):
    B, S, _ = ha.shape
    tm = TM_CONV
    nh = tm // CONV_HALO
    n_halo_blocks = S // CONV_HALO
    weights = (lw['conv_w'], lw['conv_b'], lw['conv_g'], lw['conv_beta'])
    return pl.pallas_call(
        _conv_kernel,
        grid=(B, S // tm),
        in_specs=[pl.BlockSpec((None, CONV_HALO, CONV_CH), lambda b, i: (b, jnp.maximum(i * nh - 1, 0), 0)),
                  pl.BlockSpec((None, tm, CONV_CH), lambda b, i: (b, i, 0)),
                  pl.BlockSpec((None, CONV_HALO, CONV_CH),
                               lambda b, i: (b, jnp.minimum((i + 1) * nh, n_halo_blocks - 1), 0))]
                 + [_const_spec(w.shape) for w in weights],
        out_specs=pl.BlockSpec((None, tm, CONV_CH), lambda b, i: (b, i, 0)),
        out_shape=jax.ShapeDtypeStruct((B, S, CONV_CH), BF16),
        scratch_shapes=[pltpu.VMEM((tm + 2 * CONV_HALO, CONV_CH), F32),
                        pltpu.VMEM((SUBLANES, tm + 2 * CONV_HALO - SUBLANES, CONV_CH), F32)],
        compiler_params=pltpu.CompilerParams(dimension_semantics=("parallel", "parallel"),
                                             vmem_limit_bytes=VMEM_LIMIT),
        name="conv_module",
    )(ha, ha, ha, *weights)


def _mla_kernel(q_ref, k_ref, v_ref, o_ref):
    for hd in range(MLA_HEADS):
        cols = slice(hd * LANES, (hd + 1) * LANES)
        s = lax.dot_general(q_ref[:, cols], k_ref[:, cols], (((1,), (1,)), ((), ())), preferred_element_type=F32)
        p = jnp.exp2(s - jnp.max(s, axis=-1, keepdims=True))
        o = jnp.dot(p.astype(BF16), v_ref[:, cols], preferred_element_type=F32)
        l = o[:, MLA_V:MLA_V + 1]
        o_ref[:, cols] = (o / l).astype(BF16)


def _mla_attention(q, k, v):
    B, S, W = q.shape
    tq = min(S, max(LANES, MLA_SCORE_ELEMS // S))
    return pl.pallas_call(
        _mla_kernel,
        grid=(B, S // tq),
        in_specs=[pl.BlockSpec((None, tq, W), lambda b, i: (b, i, 0)),
                  pl.BlockSpec((None, S, W), lambda b, i: (b, 0, 0)),
                  pl.BlockSpec((None, S, W), lambda b, i: (b, 0, 0))],
        out_specs=pl.BlockSpec((None, tq, W), lambda b, i: (b, i, 0)),
        out_shape=jax.ShapeDtypeStruct((B, S, W), BF16),
        compiler_params=pltpu.CompilerParams(dimension_semantics=("parallel", "arbitrary"),
                                             vmem_limit_bytes=VMEM_LIMIT),
        name="mla_attention",
    )(q, k, v)


def _dil_kernel(q_ref, k_ref, v_ref, o_ref, lse_ref, *, tq, tk):
    rb, tqb, _ = q_ref.shape
    n = k_ref.shape[1]
    i_base = pl.program_id(2) * tqb
    lane = lax.broadcasted_iota(jnp.int32, (tq, DIL_W), 1)
    q_head = (lane % 64) // 16
    v_head = lane // DIL_HEAD_DIM
    for r in range(rb):
        for t in range(tqb // tq):
            i0 = i_base + t * tq
            rows = slice(t * tq, (t + 1) * tq)
            if tk == n:
                start = 0
                kw, vw = k_ref[r], v_ref[r]
            else:
                start = pl.multiple_of(jnp.clip(i0 - DIL_SIDE, 0, n - tk), DIL_SIDE)
                kw, vw = k_ref[r, pl.ds(start, tk), :], v_ref[r, pl.ds(start, tk), :]
            q = q_ref[r, rows, :]
            row = i0 + lax.broadcasted_iota(jnp.int32, (tq, tk), 0)
            col = start + lax.broadcasted_iota(jnp.int32, (tq, tk), 1)
            valid = jnp.abs(row - col) <= DIL_SIDE
            out = jnp.zeros((tq, DIL_W), F32)
            lse = jnp.zeros((tq, DIL_W), F32)
            for hd in range(DIL_HEADS):
                qh = jnp.where(q_head == hd, q, jnp.zeros_like(q))
                s = lax.dot_general(qh, kw, (((1,), (1,)), ((), ())), preferred_element_type=F32)
                s = jnp.where(valid, s, NEG_INF)
                m = jnp.max(s, axis=-1, keepdims=True)
                p = jnp.exp2(s - m)
                l = jnp.sum(p, axis=-1, keepdims=True)
                o = jnp.dot(p.astype(BF16), vw, preferred_element_type=F32)
                out = jnp.where(v_head == hd, o / l, out)
                lse = jnp.where(v_head == hd, m * LN2 + jnp.log(l), lse)
            o_ref[r, rows, :] = out
            lse_ref[r, rows, :] = lse


def _dil_group(q, k, v, g):
    B, d, n, _ = q.shape
    tq = min(TQ_DIL, n)
    tk = min(n, tq + 2 * DIL_SIDE)
    tqb = min(DIL_ROWS_PER_STEP, n)
    rb = min(d, max(1, DIL_ROWS_PER_STEP // tqb))
    q_spec = pl.BlockSpec((None, rb, tqb, DIL_W), lambda b, r, i: (b, r, i, 0))
    kv_spec = pl.BlockSpec((None, rb, n, DIL_W), lambda b, r, i: (b, r, 0, 0))
    return pl.pallas_call(
        functools.partial(_dil_kernel, tq=tq, tk=tk),
        grid=(B, d // rb, n // tqb),
        in_specs=[q_spec, kv_spec, kv_spec],
        out_specs=[q_spec, q_spec],
        out_shape=[jax.ShapeDtypeStruct((B, d, n, DIL_W), F32)] * 2,
        compiler_params=pltpu.CompilerParams(dimension_semantics=("parallel", "parallel", "arbitrary"),
                                             vmem_limit_bytes=VMEM_LIMIT),
        name=f"dil_attention_g{g}",
    )(q, k, v)


def _merge_kernel(x_ref, ca_ref, yb_ref, om_ref, o0_ref, l0_ref, o1_ref, l1_ref, o2_ref, l2_ref,
                  g_attn_ref, w_g_ref, woa_ref, wob_ref, woc_ref, wod_ref, w_out_ref, y_ref, slab_ref):
    x = x_ref[...]
    tm = x.shape[0]
    h = (x * _rms_scale(x, D_MODEL) * g_attn_ref[...]).astype(BF16)

    def natural(ref, base):
        d = ref.shape[0]
        if d == 1:
            return ref[0]
        parts = []
        for part in range(DIL_W // LANES):
            slab = slab_ref.at[base + part]
            for r in range(d):
                slab[pl.ds(r, tm // d, stride=d), :] = ref[r, :, part * LANES:(part + 1) * LANES]
            parts.append(slab[...])
        return jnp.concatenate(parts, axis=1)

    o0, l0 = natural(o0_ref, 0), natural(l0_ref, 2)
    o1, l1 = natural(o1_ref, 4), natural(l1_ref, 6)
    o2, l2 = natural(o2_ref, 8), natural(l2_ref, 10)
    m = jnp.maximum(jnp.maximum(l0, l1), l2)
    e0, e1, e2 = jnp.exp(l0 - m), jnp.exp(l1 - m), jnp.exp(l2 - m)
    od = ((e0 * o0 + e1 * o1 + e2 * o2) / (e0 + e1 + e2)).astype(BF16)
    branches = ((ca_ref[...], woa_ref), (yb_ref[...], wob_ref), (om_ref[...], woc_ref), (od, wod_ref))
    merged = jnp.zeros(x.shape, F32)
    for i, (act, wo_ref) in enumerate(branches):
        gate = _sigmoid(jnp.dot(h, w_g_ref[:, i * D_MODEL:(i + 1) * D_MODEL], preferred_element_type=F32))
        merged = merged + gate * jnp.dot(act, wo_ref[...], preferred_element_type=F32)
    y_ref[...] = x + jnp.dot(merged.astype(BF16), w_out_ref[...], preferred_element_type=F32)


def _merge(x, ca, yb, om, dil, lw):
    B, S, _ = x.shape
    tm = TM_MERGE
    tok_spec = lambda w: pl.BlockSpec((None, tm, w), lambda b, i: (b, i, 0))
    acts = (x, ca, yb, om)
    dil_acts = tuple(a for pair in dil for a in pair)
    dil_spec = lambda d: pl.BlockSpec((None, d, tm // d, DIL_W), lambda b, i: (b, 0, i, 0))
    weights = (lw['g_attn'], lw['w_g'], lw['wo_a'], lw['wo_b'], lw['wo_c'], lw['wo_d'], lw['w_out'])
    return pl.pallas_call(
        _merge_kernel,
        grid=(B, S // tm),
        in_specs=[tok_spec(a.shape[-1]) for a in acts] + [dil_spec(a.shape[1]) for a in dil_acts]
                 + [_const_spec(w.shape) for w in weights],
        out_specs=tok_spec(D_MODEL),
        out_shape=jax.ShapeDtypeStruct((B, S, D_MODEL), F32),
        scratch_shapes=[pltpu.VMEM((2 * N_DIL * DIL_W // LANES, tm, LANES), F32)],
        compiler_params=pltpu.CompilerParams(dimension_semantics=("parallel", "parallel"),
                                             vmem_limit_bytes=VMEM_LIMIT),
        name="merge",
    )(*acts, *dil_acts, *weights)


def _ffn_kernel(prev_ref, cur_ref, next_ref, g_ref, wa_ref, wb_ref, cwa_ref, cwb_ref, cba_ref, cbb_ref, wd_ref,
                y_ref, h_ref, ua_ref, ub_ref, act_ref):
    tm = cur_ref.shape[0]
    i = pl.program_id(1)
    last = pl.num_programs(1) - 1

    def normed(x):
        return (x * _rms_scale(x, D_MODEL) * g_ref[...]).astype(BF16)

    h_ref[0:FFN_HALO, :] = jnp.where(i > 0, normed(prev_ref[...]), jnp.zeros((FFN_HALO, D_MODEL), BF16))
    h_ref[FFN_HALO:FFN_HALO + tm, :] = normed(cur_ref[...])
    h_ref[FFN_HALO + tm:, :] = jnp.where(i < last, normed(next_ref[...]), jnp.zeros((FFN_HALO, D_MODEL), BF16))

    def conv3(u_ref, cw, cb):
        return (u_ref[FFN_HALO - 1:FFN_HALO - 1 + tm, :] * cw[0:1, :] + u_ref[FFN_HALO:FFN_HALO + tm, :] * cw[1:2, :]
                + u_ref[FFN_HALO + 1:FFN_HALO + 1 + tm, :] * cw[2:3, :] + cb)

    for j in range(wa_ref.shape[0]):
        slot = j % 2
        he = h_ref[...]
        ua_ref[slot] = jnp.dot(he, wa_ref[j], preferred_element_type=F32)
        ub_ref[slot] = jnp.dot(he, wb_ref[j], preferred_element_type=F32)
        a = conv3(ua_ref.at[slot], cwa_ref[j], cba_ref[j])
        b = conv3(ub_ref.at[slot], cwb_ref[j], cbb_ref[j])
        act_ref[:, j * FFN_CHUNK:(j + 1) * FFN_CHUNK] = (a * _sigmoid(a) * b).astype(BF16)
    y_ref[...] = cur_ref[...] + jnp.dot(act_ref[...], wd_ref[...], preferred_element_type=F32)


def _ffn(x, lw):
    B, S, _ = x.shape
    tm = TM_FFN
    nh = tm // FFN_HALO
    n_halo_blocks = S // FFN_HALO
    weights = (lw['g_ffn'], lw['w_up_a'], lw['w_up_b'], lw['cw_a'], lw['cw_b'], lw['cb_a'], lw['cb_b'], lw['w_down'])
    return pl.pallas_call(
        _ffn_kernel,
        grid=(B, S // tm),
        in_specs=[pl.BlockSpec((None, FFN_HALO, D_MODEL), lambda b, i: (b, jnp.maximum(i * nh - 1, 0), 0)),
                  pl.BlockSpec((None, tm, D_MODEL), lambda b, i: (b, i, 0)),
                  pl.BlockSpec((None, FFN_HALO, D_MODEL),
                               lambda b, i: (b, jnp.minimum((i + 1) * nh, n_halo_blocks - 1), 0))]
                 + [_const_spec(w.shape) for w in weights],
        out_specs=pl.BlockSpec((None, tm, D_MODEL), lambda b, i: (b, i, 0)),
        out_shape=jax.ShapeDtypeStruct((B, S, D_MODEL), F32),
        scratch_shapes=[pltpu.VMEM((tm + 2 * FFN_HALO, D_MODEL), BF16),
                        pltpu.VMEM((2, tm + 2 * FFN_HALO, FFN_CHUNK), F32),
                        pltpu.VMEM((2, tm + 2 * FFN_HALO, FFN_CHUNK), F32),
                        pltpu.VMEM((tm, D_FF), BF16)],
        compiler_params=pltpu.CompilerParams(dimension_semantics=("parallel", "parallel"),
                                             vmem_limit_bytes=VMEM_LIMIT),
        name="conv_ffn",
    )(x, x, x, *weights)


def _mla_lane_map():
    half = MLA_ROPE // 2
    dim = -np.ones(LANES, np.int64)
    freq = -np.ones(LANES, np.int64)
    sign = np.zeros(LANES, np.float32)
    nope_lanes = list(range(half, LANES // 2)) + list(range(LANES // 2 + half, MLA_DQK))
    dim[nope_lanes] = np.arange(MLA_NOPE)
    for i in range(half):
        dim[i], freq[i], sign[i] = MLA_NOPE + i, i, -1.0
        dim[LANES // 2 + i], freq[LANES // 2 + i], sign[LANES // 2 + i] = MLA_NOPE + half + i, i, 1.0
    return dim, freq, sign


def _dil_lane_map():
    half = ROPE_DIMS // 2
    src = np.zeros(DIL_W, np.int64)
    freq = -np.ones(DIL_W, np.int64)
    sign = np.zeros(DIL_W, np.float32)
    for lane in range(DIL_W):
        col, l = divmod(lane, LANES)
        blk, hd, i = l // 64, (l % 64) // 16, l % 16
        if col == 0 and i < half:
            d = blk * half + i
            freq[lane], sign[lane] = i, (-1.0 if blk == 0 else 1.0)
        elif col == 0:
            d = ROPE_DIMS + blk * half + (i - half)
        else:
            d = 2 * ROPE_DIMS + blk * 16 + i
        src[lane] = hd * DIL_HEAD_DIM + d
    return src, freq, sign


def _rope_tables(seq, dims, theta, freq, sign):
    inv = jnp.exp(-math.log(theta) * jnp.arange(0, dims, 2, dtype=F32) / dims)
    ang = jnp.arange(seq, dtype=F32)[:, None] * inv[None, :]
    cos, sin = jnp.cos(ang), jnp.sin(ang)
    on = jnp.asarray(freq >= 0)[None, :]
    idx = np.maximum(freq, 0)
    return (jnp.where(on, cos[:, idx], 1.0), jnp.where(on, sin[:, idx] * jnp.asarray(sign)[None, :], 0.0))


def _place_lanes(w, dim):
    return jnp.where(jnp.asarray(dim >= 0)[None, :], w[:, np.maximum(dim, 0)], 0.0)


def _pad_heads(w, n_heads, width):
    k = w.shape[0]
    w = w.reshape(k, n_heads, width)
    return jnp.pad(w, ((0, 0), (0, 0), (0, LANES - width))).reshape(k, n_heads * LANES)


def _prep_layer(p, l):
    row = lambda a: a.reshape(1, -1).astype(F32)
    w_in = p['w_in'][l]
    o_c = N_A + N_B
    o_d = o_c + N_C
    o_g = o_d + N_D
    w_c = w_in[:, o_c:o_d]
    m_dim, _, _ = _mla_lane_map()
    nope_dim = np.where(m_dim < MLA_NOPE, m_dim, -1)
    rope_dim = np.where(m_dim >= MLA_NOPE, m_dim - MLA_NOPE, -1)
    w_kr = _place_lanes(w_c[:, MLA_Q_RANK + MLA_KV_RANK:], rope_dim)
    w_ukv = p['mla_w_ukv'][l].reshape(MLA_KV_RANK, MLA_HEADS, MLA_NOPE + MLA_V)
    w_uq = p['mla_w_uq'][l].reshape(MLA_Q_RANK, MLA_HEADS, MLA_DQK)
    per_head = lambda f: jnp.concatenate([f(hd) for hd in range(MLA_HEADS)], axis=1)
    d_src, _, _ = _dil_lane_map()
    w_d = w_in[:, o_d:o_g]
    n_qk = N_DIL * DIL_W
    qk_cols = np.concatenate([which * n_qk + g * DIL_W + d_src for which in range(2) for g in range(N_DIL)])
    n_chunks = D_FF // FFN_CHUNK
    w_up = p['ffn_w_up'][l]
    chunk_cols = lambda w: w.reshape(w.shape[0], n_chunks, FFN_CHUNK).transpose(1, 0, 2)
    cw, cb = p['ffn_conv_w'][l], p['ffn_conv_b'][l].reshape(1, -1)
    return {
        'g_attn': row(p['attn_norm'][l]),
        'w_ab': w_in[:, :o_c].astype(BF16),
        'w_c': jnp.concatenate([w_c[:, :MLA_Q_RANK + MLA_KV_RANK], w_kr], axis=1).astype(BF16),
        'w_d': jnp.concatenate([w_d[:, qk_cols], w_d[:, 2 * n_qk:]], axis=1).astype(BF16),
        'w_g': w_in[:, o_g:].astype(BF16),
        'conv_w': p['conv_w'][l], 'conv_b': row(p['conv_b'][l]),
        'conv_g': row(p['conv_ln_g'][l]), 'conv_beta': row(p['conv_ln_b'][l]),
        'sgu_g': row(p['sgu_ln_g'][l]), 'sgu_b': row(p['sgu_ln_b'][l]),
        'ws': p['sgu_w_s'][l].astype(BF16),
        'bs': jnp.repeat(p['sgu_b_s'][l].T, SGU_CH // SGU_GROUPS, axis=1),
        'g_cq': row(p['mla_g_cq'][l]), 'g_ckv': row(p['mla_g_ckv'][l]),
        'w_uq': per_head(lambda hd: _place_lanes(w_uq[:, hd], m_dim)).astype(BF16),
        'w_uk': per_head(lambda hd: _place_lanes(w_ukv[:, hd, :MLA_NOPE], nope_dim)).astype(BF16),
        'w_uv': _pad_heads(w_ukv[:, :, MLA_NOPE:].reshape(MLA_KV_RANK, -1), MLA_HEADS, MLA_V).astype(BF16),
        'gq_mla': _place_lanes(row(p['mla_g_qn'][l]) * (MLA_DQK ** -0.5 * LOG2E), m_dim),
        'gk_mla': _place_lanes(row(p['mla_g_kn'][l]), m_dim),
        'gq_dil': row(p['dil_g_qn'][l])[:, d_src % DIL_HEAD_DIM] * (DIL_HEAD_DIM ** -0.5 * LOG2E),
        'gk_dil': row(p['dil_g_kn'][l])[:, d_src % DIL_HEAD_DIM],
        'wo_a': p['conv_w_o'][l].astype(BF16), 'wo_b': p['sgu_w_o'][l].astype(BF16),
        'wo_c': _pad_heads(p['mla_w_o'][l].T, MLA_HEADS, MLA_V).T.astype(BF16),
        'wo_d': p['dil_w_o'][l].astype(BF16),
        'w_out': p['w_out'][l].astype(BF16),
        'g_ffn': row(p['ffn_norm'][l]),
        'w_up_a': chunk_cols(w_up[:, :D_FF]).astype(BF16), 'w_up_b': chunk_cols(w_up[:, D_FF:]).astype(BF16),
        'cw_a': chunk_cols(cw[:, :D_FF]), 'cw_b': chunk_cols(cw[:, D_FF:]),
        'cb_a': chunk_cols(cb[:, :D_FF]), 'cb_b': chunk_cols(cb[:, D_FF:]),
        'w_down': p['ffn_w_down'][l].astype(BF16),
    }


def _trunk(x, layers):
    S = x.shape[1]
    _, m_freq, m_sign = _mla_lane_map()
    _, d_freq, d_sign = _dil_lane_map()
    tabs = (_rope_tables(S, MLA_ROPE, MLA_THETA, m_freq, m_sign)
            + _rope_tables(S, ROPE_DIMS, ROPE_THETA, d_freq[:LANES], d_sign[:LANES]))
    for lw in layers:
        (ha, yb, qm, km, vm), qd, kd, vd = _inproj(x, tabs, lw)
        ca = _conv_module(ha, lw)
        om = _mla_attention(qm, km, vm)
        dil = [_dil_group(qd[g], kd[g], vd[g], g) for g in range(N_DIL)]
        x = _merge(x, ca, yb, om, dil, lw)
        x = _ffn(x, lw)
    return x


def kernel(x_prompt, x_sample, attn_norm, w_in, conv_w, conv_b, conv_ln_g, conv_ln_b, conv_w_o, sgu_ln_g, sgu_ln_b, sgu_w_s, sgu_b_s, sgu_w_o, mla_g_cq, mla_g_ckv, mla_w_uq, mla_w_ukv, mla_g_qn, mla_g_kn, mla_w_o, dil_g_qn, dil_g_kn, dil_w_o, w_out, ffn_norm, ffn_w_up, ffn_conv_w, ffn_conv_b, ffn_w_down):
    p = dict(attn_norm=attn_norm, w_in=w_in, conv_w=conv_w, conv_b=conv_b, conv_ln_g=conv_ln_g, conv_ln_b=conv_ln_b,
             conv_w_o=conv_w_o, sgu_ln_g=sgu_ln_g, sgu_ln_b=sgu_ln_b, sgu_w_s=sgu_w_s, sgu_b_s=sgu_b_s,
             sgu_w_o=sgu_w_o, mla_g_cq=mla_g_cq, mla_g_ckv=mla_g_ckv, mla_w_uq=mla_w_uq, mla_w_ukv=mla_w_ukv,
             mla_g_qn=mla_g_qn, mla_g_kn=mla_g_kn, mla_w_o=mla_w_o, dil_g_qn=dil_g_qn, dil_g_kn=dil_g_kn,
             dil_w_o=dil_w_o, w_out=w_out, ffn_norm=ffn_norm, ffn_w_up=ffn_w_up, ffn_conv_w=ffn_conv_w,
             ffn_conv_b=ffn_conv_b, ffn_w_down=ffn_w_down)
    layers = [_prep_layer(p, l) for l in range(DEPTH)]
    return (_trunk(x_prompt, layers), _trunk(x_sample, layers))
```

```python
import functools
import math

import jax
import jax.numpy as jnp
import numpy as np
from jax import lax
from jax.experimental import pallas as pl
from jax.experimental.pallas import tpu as pltpu

F32 = jnp.float32
BF16 = jnp.bfloat16

D_MODEL = 1024
DEPTH = 2
EPS = 1e-6
NEG_INF = -1e30
CONV_CH = 256
CONV_WIDTH = 31
SGU_CH = 256
SGU_GROUPS = 4
SGU_CHUNK = 128
MLA_HEADS = 4
MLA_Q_RANK = 256
MLA_KV_RANK = 128
MLA_NOPE = 64
MLA_ROPE = 32
MLA_V = 64
MLA_THETA = 10000.0
DIL_GROUPS = ((128, 1), (512, 4), (2048, 16))
N_DIL = 3
DIL_HEADS = 4
DIL_HEAD_DIM = 64
ROPE_THETA = 500000.0
ROPE_DIMS = DIL_HEAD_DIM // 4
N_BRANCH = 4
D_FF = 2816
N_A = 2 * CONV_CH
N_B = 2 * SGU_CH
N_C = MLA_Q_RANK + MLA_KV_RANK + MLA_ROPE
N_D = 3 * N_DIL * DIL_HEADS * DIL_HEAD_DIM
DIL_SIDE = 64
MLA_DQK = MLA_NOPE + MLA_ROPE
DIL_W = DIL_HEADS * DIL_HEAD_DIM
LOG2E = math.log2(math.e)
LN2 = math.log(2.0)

LANES = 128
SUBLANES = 8
VMEM_LIMIT = 56 * 1024 * 1024

TM_PROJ = 512
SUB_PROJ = 256
CONV_HALO = 16
MLA_SCORE_ELEMS = 1 << 20
TQ_DIL = 256
DIL_ROWS_PER_STEP = 512
TM_MERGE = 256
MERGE_CHUNK = 1024
TM_FFN = 512
FFN_HALO = 16
FFN_CHUNK = 256


def _const_spec(shape):
    nd = len(shape)
    return pl.BlockSpec(shape, lambda *_: (0,) * nd, pipeline_mode=pl.Buffered(1))


def _sigmoid(x):
    return 1.0 / (1.0 + jnp.exp(-x))


def _rms_scale(x, n):
    return lax.rsqrt(jnp.sum(x * x, axis=-1, keepdims=True) * (1.0 / n) + EPS)


def _layer_norm(x, g, b):
    mu = jnp.mean(x, axis=-1, keepdims=True)
    xc = x - mu
    var = jnp.mean(xc * xc, axis=-1, keepdims=True)
    return xc * lax.rsqrt(var + EPS) * g + b


def _rope(x, cos, sin):
    return x * cos + pltpu.roll(x, LANES // 2, 1) * sin


def _inproj_kernel(x_ref, mcos_ref, msin_ref, dcos_ref, dsin_ref,
                   g_attn_ref, w_ab_ref, w_c_ref, w_d_ref,
                   sgu_g_ref, sgu_b_ref, ws_ref, bs_ref,
                   g_cq_ref, g_ckv_ref, w_uq_ref, w_uk_ref, w_uv_ref, gq_mla_ref, gk_mla_ref,
                   gq_dil_ref, gk_dil_ref,
                   ha_ref, yb_ref, qm_ref, km_ref, vm_ref, *dil_and_scratch):
    dil_refs, slab_ref = dil_and_scratch[:3 * N_DIL], dil_and_scratch[3 * N_DIL]
    tm = x_ref.shape[0]
    x = x_ref[...]
    h = (x * _rms_scale(x, D_MODEL) * g_attn_ref[...]).astype(BF16)

    dcos, dsin = dcos_ref[...], dsin_ref[...]
    lane_head = (lax.broadcasted_iota(jnp.int32, (tm, LANES), 1) % 64) // 16
    head_masks = [lane_head == hd for hd in range(DIL_HEADS)]
    slab_i = 0
    for which, gain_ref in enumerate((gq_dil_ref, gk_dil_ref, None)):
        for g in range(N_DIL):
            base = (which * N_DIL + g) * DIL_W
            z = jnp.dot(h, w_d_ref[:, base:base + DIL_W], preferred_element_type=F32)
            x0, x1 = z[:, :LANES], z[:, LANES:]
            if gain_ref is not None:
                sq = x0 * x0 + x1 * x1
                ms = jnp.zeros((tm, LANES), F32)
                for hd in range(DIL_HEADS):
                    ms = jnp.where(head_masks[hd],
                                   jnp.sum(jnp.where(head_masks[hd], sq, 0.0), axis=-1, keepdims=True), ms)
                scale = lax.rsqrt(ms * (1.0 / DIL_HEAD_DIM) + EPS)
                x0 = _rope(x0 * scale * gain_ref[:, :LANES], dcos, dsin)
                x1 = x1 * scale * gain_ref[:, LANES:]
            out_ref = dil_refs[which * N_DIL + g]
            d = out_ref.shape[0]
            for part, xc in enumerate((x0, x1)):
                lanes = slice(part * LANES, (part + 1) * LANES)
                if d == 1:
                    out_ref[0, :, lanes] = xc.astype(BF16)
                else:
                    slab = slab_ref.at[slab_i]
                    slab_i += 1
                    slab[...] = xc
                    for r in range(d):
                        out_ref[r, :, lanes] = slab[pl.ds(r, tm // d, stride=d), :].astype(BF16)

    z_c = jnp.dot(h, w_c_ref[...], preferred_element_type=F32)
    c_q = z_c[:, :MLA_Q_RANK]
    c_q = (c_q * _rms_scale(c_q, MLA_Q_RANK) * g_cq_ref[...]).astype(BF16)
    c_kv = z_c[:, MLA_Q_RANK:MLA_Q_RANK + MLA_KV_RANK]
    c_kv = (c_kv * _rms_scale(c_kv, MLA_KV_RANK) * g_ckv_ref[...]).astype(BF16)
    k_rope = z_c[:, MLA_Q_RANK + MLA_KV_RANK:]
    q = jnp.dot(c_q, w_uq_ref[...], preferred_element_type=F32)
    kn = jnp.dot(c_kv, w_uk_ref[...], preferred_element_type=F32)
    v_lane = lax.broadcasted_iota(jnp.int32, (tm, MLA_HEADS * LANES), 1) % LANES
    vm_ref[...] = jnp.where(v_lane == MLA_V, 1.0,
                            jnp.dot(c_kv, w_uv_ref[...], preferred_element_type=F32)).astype(BF16)
    mcos, msin = mcos_ref[...], msin_ref[...]
    for hd in range(MLA_HEADS):
        cols = slice(hd * LANES, (hd + 1) * LANES)
        qh = q[:, cols]
        qh = qh * _rms_scale(qh, MLA_DQK) * gq_mla_ref[...]
        qm_ref[:, cols] = _rope(qh, mcos, msin).astype(BF16)
        kh = kn[:, cols] + k_rope
        kh = kh * _rms_scale(kh, MLA_DQK) * gk_mla_ref[...]
        km_ref[:, cols] = _rope(kh, mcos, msin).astype(BF16)

    z_a = jnp.dot(h, w_ab_ref[:, :N_A], preferred_element_type=F32)
    ha_ref[...] = z_a[:, :CONV_CH] * _sigmoid(z_a[:, CONV_CH:])
    zb = jnp.dot(h, w_ab_ref[:, N_A:], preferred_element_type=F32)
    zb = 0.5 * zb * (1.0 + jnp.tanh(math.sqrt(2.0 / math.pi) * (zb + 0.044715 * (zb * zb * zb))))
    u = zb[:, :SGU_CH]
    v = _layer_norm(zb[:, SGU_CH:], sgu_g_ref[...], sgu_b_ref[...]).astype(BF16)
    lane = lax.broadcasted_iota(jnp.int32, (SGU_CHUNK, SGU_CH), 1)
    gw = SGU_CH // SGU_GROUPS
    for c in range(tm // SGU_CHUNK):
        rows = slice(c * SGU_CHUNK, (c + 1) * SGU_CHUNK)
        vc = v[rows]
        sv = jnp.dot(ws_ref[SGU_GROUPS - 1], vc, preferred_element_type=F32)
        for g in range(SGU_GROUPS - 2, -1, -1):
            sv = jnp.where(lane < (g + 1) * gw, jnp.dot(ws_ref[g], vc, preferred_element_type=F32), sv)
        yb_ref[rows, :] = (u[rows] * (sv + bs_ref[...])).astype(BF16)


N_PROJ_TOK_OUT = 5
N_PROJ_SLABS = 3 * N_DIL * DIL_W // LANES


def _inproj_tile_kernel(*refs, n_rowwise_in, n_weights):
    rowwise_in = refs[:n_rowwise_in]
    weights = refs[n_rowwise_in:n_rowwise_in + n_weights]
    outs = refs[n_rowwise_in + n_weights:]
    tok_outs, dil_outs, slab_ref = outs[:N_PROJ_TOK_OUT], outs[N_PROJ_TOK_OUT:-1], outs[-1]
    for sub in range(rowwise_in[0].shape[0] // SUB_PROJ):
        rows = pl.ds(sub * SUB_PROJ, SUB_PROJ)
        dil_views = []
        for r in dil_outs:
            per_class = SUB_PROJ // r.shape[0]
            dil_views.append(r.at[:, pl.ds(sub * per_class, per_class)])
        _inproj_kernel(*[r.at[rows] for r in rowwise_in], *weights, *[r.at[rows] for r in tok_outs], *dil_views,
                       slab_ref.at[pl.ds(sub * N_PROJ_SLABS, N_PROJ_SLABS)])


def _inproj(x, tabs, lw):
    B, S, _ = x.shape
    tm = TM_PROJ
    grid = (B, S // tm)
    tok = lambda w, dt: jax.ShapeDtypeStruct((B, S, w), dt)
    tok_spec = lambda w: pl.BlockSpec((None, tm, w), lambda b, i: (b, i, 0))
    tab_spec = pl.BlockSpec((tm, LANES), lambda b, i: (i, 0))
    weights = (lw['g_attn'], lw['w_ab'], lw['w_c'], lw['w_d'], lw['sgu_g'], lw['sgu_b'], lw['ws'], lw['bs'],
               lw['g_cq'], lw['g_ckv'], lw['w_uq'], lw['w_uk'], lw['w_uv'], lw['gq_mla'], lw['gk_mla'],
               lw['gq_dil'], lw['gk_dil'])
    dils = [d for _ in range(3) for (_, d) in DIL_GROUPS]
    outs = pl.pallas_call(
        functools.partial(_inproj_tile_kernel, n_rowwise_in=1 + len(tabs), n_weights=len(weights)),
        grid=grid,
        in_specs=[tok_spec(D_MODEL)] + [tab_spec] * len(tabs) + [_const_spec(w.shape) for w in weights],
        out_specs=[tok_spec(CONV_CH), tok_spec(SGU_CH), tok_spec(4 * LANES), tok_spec(4 * LANES), tok_spec(4 * LANES)]
                  + [pl.BlockSpec((None, d, tm // d, DIL_W), lambda b, i: (b, 0, i, 0)) for d in dils],
        out_shape=[tok(CONV_CH, F32), tok(SGU_CH, BF16), tok(4 * LANES, BF16), tok(4 * LANES, BF16), tok(4 * LANES, BF16)]
                  + [jax.ShapeDtypeStruct((B, d, S // d, DIL_W), BF16) for d in dils],
        scratch_shapes=[pltpu.VMEM((tm // SUB_PROJ * N_PROJ_SLABS, SUB_PROJ, LANES), F32)],
        compiler_params=pltpu.CompilerParams(dimension_semantics=("parallel", "parallel"),
                                             vmem_limit_bytes=VMEM_LIMIT),
        name="inproj",
    )(x, *tabs, *weights)
    return outs[:5], outs[5:5 + N_DIL], outs[5 + N_DIL:5 + 2 * N_DIL], outs[5 + 2 * N_DIL:]


def _conv_module_tile(prev_ref, cur_ref, next_ref, w_ref, b_ref, g_ref, beta_ref, ext_ref, shift_ref):
    tm = cur_ref.shape[0]
    i = pl.program_id(1)
    last = pl.num_programs(1) - 1
    ext_ref[0:CONV_HALO, :] = jnp.where(i > 0, prev_ref[...], 0.0)
    ext_ref[CONV_HALO:CONV_HALO + tm, :] = cur_ref[...]
    ext_ref[CONV_HALO + tm:, :] = jnp.where(i < last, next_ref[...], 0.0)
    n_shift = shift_ref.shape[1]
    for s in range(1, SUBLANES):
        shift_ref[s] = ext_ref[s:s + n_shift, :]
    acc = jnp.zeros((tm, CONV_CH), F32) + b_ref[...]
    base = CONV_HALO - CONV_WIDTH // 2
    for k in range(CONV_WIDTH):
        a, s = divmod(base + k, SUBLANES)
        src = ext_ref if s == 0 else shift_ref.at[s]
        acc = acc + src[a * SUBLANES:a * SUBLANES + tm, :] * w_ref[k:k + 1, :]
    y = _layer_norm(acc, g_ref[...], beta_ref[...])
    return (y * _sigmoid(y)).astype(BF16)


def _halo_specs(tm, halo, width, seq):
    nh = tm // halo
    n_halo_blocks = seq // halo
    return [pl.BlockSpec((None, halo, width), lambda b, i: (b, jnp.maximum(i * nh - 1, 0), 0)),
            pl.BlockSpec((None, tm, width), lambda b, i: (b, i, 0)),
            pl.BlockSpec((None, halo, width), lambda b, i: (b, jnp.minimum((i + 1) * nh, n_halo_blocks - 1), 0))]


def _mla_kernel(q_ref, k_ref, v_ref, o_ref):
    for hd in range(MLA_HEADS):
        cols = slice(hd * LANES, (hd + 1) * LANES)
        s = lax.dot_general(q_ref[:, cols], k_ref[:, cols], (((1,), (1,)), ((), ())), preferred_element_type=F32)
        p = jnp.exp2(s - jnp.max(s, axis=-1, keepdims=True))
        o = jnp.dot(p.astype(BF16), v_ref[:, cols], preferred_element_type=F32)
        l = o[:, MLA_V:MLA_V + 1]
        o_ref[:, cols] = (o / l).astype(BF16)


def _mla_attention(q, k, v):
    B, S, W = q.shape
    tq = min(S, max(LANES, MLA_SCORE_ELEMS // S))
    return pl.pallas_call(
        _mla_kernel,
        grid=(B, S // tq),
        in_specs=[pl.BlockSpec((None, tq, W), lambda b, i: (b, i, 0)),
                  pl.BlockSpec((None, S, W), lambda b, i: (b, 0, 0)),
                  pl.BlockSpec((None, S, W), lambda b, i: (b, 0, 0))],
        out_specs=pl.BlockSpec((None, tq, W), lambda b, i: (b, i, 0)),
        out_shape=jax.ShapeDtypeStruct((B, S, W), BF16),
        compiler_params=pltpu.CompilerParams(dimension_semantics=("parallel", "arbitrary"),
                                             vmem_limit_bytes=VMEM_LIMIT),
        name="mla_attention",
    )(q, k, v)


def _dil_kernel(q_ref, k_ref, v_ref, o_ref, lse_ref, *, tq, tk):
    rb, tqb, _ = q_ref.shape
    n = k_ref.shape[1]
    i_base = pl.program_id(2) * tqb
    lane = lax.broadcasted_iota(jnp.int32, (tq, DIL_W), 1)
    q_head = (lane % 64) // 16
    low_half = lax.broadcasted_iota(jnp.int32, (tq, LANES), 1) < DIL_HEAD_DIM
    for r in range(rb):
        for t in range(tqb // tq):
            i0 = i_base + t * tq
            rows = slice(t * tq, (t + 1) * tq)
            if tk == n:
                start = 0
                kw, vw = k_ref[r], v_ref[r]
            else:
                start = pl.multiple_of(jnp.clip(i0 - DIL_SIDE, 0, n - tk), DIL_SIDE)
                kw, vw = k_ref[r, pl.ds(start, tk), :], v_ref[r, pl.ds(start, tk), :]
            q = q_ref[r, rows, :]
            row = i0 + lax.broadcasted_iota(jnp.int32, (tq, tk), 0)
            col = start + lax.broadcasted_iota(jnp.int32, (tq, tk), 1)
            valid = jnp.abs(row - col) <= DIL_SIDE
            outs, lses = [], []
            for hd in range(DIL_HEADS):
                qh = jnp.where(q_head == hd, q, jnp.zeros_like(q))
                s = lax.dot_general(qh, kw, (((1,), (1,)), ((), ())), preferred_element_type=F32)
                s = jnp.where(valid, s, NEG_INF)
                m = jnp.max(s, axis=-1, keepdims=True)
                p = jnp.exp2(s - m)
                l = jnp.sum(p, axis=-1, keepdims=True)
                col = slice((hd // 2) * LANES, (hd // 2 + 1) * LANES)
                o = jnp.dot(p.astype(BF16), vw, preferred_element_type=F32)
                outs.append(o[:, col] / l)
                lses.append(m * LN2 + jnp.log(l))
            for c in range(DIL_W // LANES):
                col = slice(c * LANES, (c + 1) * LANES)
                o_ref[r, rows, col] = jnp.where(low_half, outs[2 * c], outs[2 * c + 1])
                lse_ref[r, rows, col] = jnp.where(low_half, lses[2 * c], lses[2 * c + 1])


def _dil_group(q, k, v, g):
    B, d, n, _ = q.shape
    tq = min(TQ_DIL, n)
    tk = min(n, tq + 2 * DIL_SIDE)
    tqb = min(DIL_ROWS_PER_STEP, n)
    rb = min(d, max(1, DIL_ROWS_PER_STEP // tqb))
    q_spec = pl.BlockSpec((None, rb, tqb, DIL_W), lambda b, r, i: (b, r, i, 0))
    kv_spec = pl.BlockSpec((None, rb, n, DIL_W), lambda b, r, i: (b, r, 0, 0))
    return pl.pallas_call(
        functools.partial(_dil_kernel, tq=tq, tk=tk),
        grid=(B, d // rb, n // tqb),
        in_specs=[q_spec, kv_spec, kv_spec],
        out_specs=[q_spec, q_spec],
        out_shape=[jax.ShapeDtypeStruct((B, d, n, DIL_W), F32)] * 2,
        compiler_params=pltpu.CompilerParams(dimension_semantics=("parallel", "parallel", "arbitrary"),
                                             vmem_limit_bytes=VMEM_LIMIT),
        name=f"dil_attention_g{g}",
    )(q, k, v)


def _merge_kernel(x_ref, ha_prev_ref, ha_ref, ha_next_ref, yb_ref, om_ref,
                  o0_ref, l0_ref, o1_ref, l1_ref, o2_ref, l2_ref,
                  conv_w_ref, conv_b_ref, conv_g_ref, conv_beta_ref,
                  g_attn_ref, w_g_ref, woa_ref, wob_ref, woc_ref, wod_ref, w_out_ref, y_ref,
                  slab_ref, ext_ref, shift_ref, merged_ref):
    x = x_ref[...]
    tm = x.shape[0]
    h = (x * _rms_scale(x, D_MODEL) * g_attn_ref[...]).astype(BF16)

    def natural(ref, base):
        d = ref.shape[0]
        if d == 1:
            return ref[0]
        parts = []
        for part in range(DIL_W // LANES):
            slab = slab_ref.at[base + part]
            for r in range(d):
                slab[pl.ds(r, tm // d, stride=d), :] = ref[r, :, part * LANES:(part + 1) * LANES]
            parts.append(slab[...])
        return jnp.concatenate(parts, axis=1)

    o0, l0 = natural(o0_ref, 0), natural(l0_ref, 2)
    o1, l1 = natural(o1_ref, 4), natural(l1_ref, 6)
    o2, l2 = natural(o2_ref, 8), natural(l2_ref, 10)
    m = jnp.maximum(jnp.maximum(l0, l1), l2)
    e0, e1, e2 = jnp.exp(l0 - m), jnp.exp(l1 - m), jnp.exp(l2 - m)
    od = ((e0 * o0 + e1 * o1 + e2 * o2) / (e0 + e1 + e2)).astype(BF16)
    ca = _conv_module_tile(ha_prev_ref, ha_ref, ha_next_ref, conv_w_ref, conv_b_ref, conv_g_ref, conv_beta_ref,
                           ext_ref, shift_ref)
    branches = ((ca, woa_ref), (yb_ref[...], wob_ref), (om_ref[...], woc_ref), (od, wod_ref))
    for c in range(D_MODEL // MERGE_CHUNK):
        cols = slice(c * MERGE_CHUNK, (c + 1) * MERGE_CHUNK)
        part = jnp.zeros((tm, MERGE_CHUNK), F32)
        for i, (act, wo_ref) in enumerate(branches):
            gcols = slice(i * D_MODEL + c * MERGE_CHUNK, i * D_MODEL + (c + 1) * MERGE_CHUNK)
            gate = _sigmoid(jnp.dot(h, w_g_ref[:, gcols], preferred_element_type=F32))
            part = part + gate * jnp.dot(act, wo_ref[:, cols], preferred_element_type=F32)
        merged_ref[:, cols] = part.astype(BF16)
    y_ref[...] = x + jnp.dot(merged_ref[...], w_out_ref[...], preferred_element_type=F32)


def _merge(x, ha, yb, om, dil, lw):
    B, S, _ = x.shape
    tm = TM_MERGE
    tok_spec = lambda w: pl.BlockSpec((None, tm, w), lambda b, i: (b, i, 0))
    dil_acts = tuple(a for pair in dil for a in pair)
    dil_spec = lambda d: pl.BlockSpec((None, d, tm // d, DIL_W), lambda b, i: (b, 0, i, 0))
    weights = (lw['conv_w'], lw['conv_b'], lw['conv_g'], lw['conv_beta'],
               lw['g_attn'], lw['w_g'], lw['wo_a'], lw['wo_b'], lw['wo_c'], lw['wo_d'], lw['w_out'])
    return pl.pallas_call(
        _merge_kernel,
        grid=(B, S // tm),
        in_specs=[tok_spec(D_MODEL)] + _halo_specs(tm, CONV_HALO, CONV_CH, S) + [tok_spec(SGU_CH), tok_spec(om.shape[-1])]
                 + [dil_spec(a.shape[1]) for a in dil_acts] + [_const_spec(w.shape) for w in weights],
        out_specs=tok_spec(D_MODEL),
        out_shape=jax.ShapeDtypeStruct((B, S, D_MODEL), F32),
        scratch_shapes=[pltpu.VMEM((2 * N_DIL * DIL_W // LANES, tm, LANES), F32),
                        pltpu.VMEM((tm + 2 * CONV_HALO, CONV_CH), F32),
                        pltpu.VMEM((SUBLANES, tm + 2 * CONV_HALO - SUBLANES, CONV_CH), F32),
                        pltpu.VMEM((tm, D_MODEL), BF16)],
        compiler_params=pltpu.CompilerParams(dimension_semantics=("parallel", "parallel"),
                                             vmem_limit_bytes=VMEM_LIMIT),
        name="merge",
    )(x, ha, ha, ha, yb, om, *dil_acts, *weights)


def _ffn_kernel(prev_ref, cur_ref, next_ref, g_ref, wa_ref, wb_ref, cwa_ref, cwb_ref, cba_ref, cbb_ref, wd_ref,
                y_ref, h_ref, ua_ref, ub_ref, act_ref):
    tm = cur_ref.shape[0]
    i = pl.program_id(1)
    last = pl.num_programs(1) - 1

    def normed(x):
        return (x * _rms_scale(x, D_MODEL) * g_ref[...]).astype(BF16)

    h_ref[0:FFN_HALO, :] = jnp.where(i > 0, normed(prev_ref[...]), jnp.zeros((FFN_HALO, D_MODEL), BF16))
    h_ref[FFN_HALO:FFN_HALO + tm, :] = normed(cur_ref[...])
    h_ref[FFN_HALO + tm:, :] = jnp.where(i < last, normed(next_ref[...]), jnp.zeros((FFN_HALO, D_MODEL), BF16))

    def conv3(u_ref, cw, cb):
        return (u_ref[FFN_HALO - 1:FFN_HALO - 1 + tm, :] * cw[0:1, :] + u_ref[FFN_HALO:FFN_HALO + tm, :] * cw[1:2, :]
                + u_ref[FFN_HALO + 1:FFN_HALO + 1 + tm, :] * cw[2:3, :] + cb)

    for j in range(wa_ref.shape[0]):
        slot = j % 2
        he = h_ref[...]
        ua_ref[slot] = jnp.dot(he, wa_ref[j], preferred_element_type=F32)
        ub_ref[slot] = jnp.dot(he, wb_ref[j], preferred_element_type=F32)
        a = conv3(ua_ref.at[slot], cwa_ref[j], cba_ref[j])
        b = conv3(ub_ref.at[slot], cwb_ref[j], cbb_ref[j])
        act_ref[:, j * FFN_CHUNK:(j + 1) * FFN_CHUNK] = (a * _sigmoid(a) * b).astype(BF16)
    y_ref[...] = cur_ref[...] + jnp.dot(act_ref[...], wd_ref[...], preferred_element_type=F32)


def _ffn(x, lw):
    B, S, _ = x.shape
    tm = TM_FFN
    weights = (lw['g_ffn'], lw['w_up_a'], lw['w_up_b'], lw['cw_a'], lw['cw_b'], lw['cb_a'], lw['cb_b'], lw['w_down'])
    return pl.pallas_call(
        _ffn_kernel,
        grid=(B, S // tm),
        in_specs=_halo_specs(tm, FFN_HALO, D_MODEL, S) + [_const_spec(w.shape) for w in weights],
        out_specs=pl.BlockSpec((None, tm, D_MODEL), lambda b, i: (b, i, 0)),
        out_shape=jax.ShapeDtypeStruct((B, S, D_MODEL), F32),
        scratch_shapes=[pltpu.VMEM((tm + 2 * FFN_HALO, D_MODEL), BF16),
                        pltpu.VMEM((2, tm + 2 * FFN_HALO, FFN_CHUNK), F32),
                        pltpu.VMEM((2, tm + 2 * FFN_HALO, FFN_CHUNK), F32),
                        pltpu.VMEM((tm, D_FF), BF16)],
        compiler_params=pltpu.CompilerParams(dimension_semantics=("parallel", "parallel"),
                                             vmem_limit_bytes=VMEM_LIMIT),
        name="conv_ffn",
    )(x, x, x, *weights)


def _mla_lane_map():
    half = MLA_ROPE // 2
    dim = -np.ones(LANES, np.int64)
    freq = -np.ones(LANES, np.int64)
    sign = np.zeros(LANES, np.float32)
    nope_lanes = list(range(half, LANES // 2)) + list(range(LANES // 2 + half, MLA_DQK))
    dim[nope_lanes] = np.arange(MLA_NOPE)
    for i in range(half):
        dim[i], freq[i], sign[i] = MLA_NOPE + i, i, -1.0
        dim[LANES // 2 + i], freq[LANES // 2 + i], sign[LANES // 2 + i] = MLA_NOPE + half + i, i, 1.0
    return dim, freq, sign


def _dil_lane_map():
    half = ROPE_DIMS // 2
    src = np.zeros(DIL_W, np.int64)
    freq = -np.ones(DIL_W, np.int64)
    sign = np.zeros(DIL_W, np.float32)
    for lane in range(DIL_W):
        col, l = divmod(lane, LANES)
        blk, hd, i = l // 64, (l % 64) // 16, l % 16
        if col == 0 and i < half:
            d = blk * half + i
            freq[lane], sign[lane] = i, (-1.0 if blk == 0 else 1.0)
        elif col == 0:
            d = ROPE_DIMS + blk * half + (i - half)
        else:
            d = 2 * ROPE_DIMS + blk * 16 + i
        src[lane] = hd * DIL_HEAD_DIM + d
    return src, freq, sign


def _rope_tables(seq, dims, theta, freq, sign):
    inv = jnp.exp(-math.log(theta) * jnp.arange(0, dims, 2, dtype=F32) / dims)
    ang = jnp.arange(seq, dtype=F32)[:, None] * inv[None, :]
    cos, sin = jnp.cos(ang), jnp.sin(ang)
    on = jnp.asarray(freq >= 0)[None, :]
    idx = np.maximum(freq, 0)
    return (jnp.where(on, cos[:, idx], 1.0), jnp.where(on, sin[:, idx] * jnp.asarray(sign)[None, :], 0.0))


def _place_lanes(w, dim):
    return jnp.where(jnp.asarray(dim >= 0)[None, :], w[:, np.maximum(dim, 0)], 0.0)


def _pad_heads(w, n_heads, width):
    k = w.shape[0]
    w = w.reshape(k, n_heads, width)
    return jnp.pad(w, ((0, 0), (0, 0), (0, LANES - width))).reshape(k, n_heads * LANES)


def _prep_layer(p, l):
    row = lambda a: a.reshape(1, -1).astype(F32)
    w_in = p['w_in'][l]
    o_c = N_A + N_B
    o_d = o_c + N_C
    o_g = o_d + N_D
    w_c = w_in[:, o_c:o_d]
    m_dim, _, _ = _mla_lane_map()
    nope_dim = np.where(m_dim < MLA_NOPE, m_dim, -1)
    rope_dim = np.where(m_dim >= MLA_NOPE, m_dim - MLA_NOPE, -1)
    w_kr = _place_lanes(w_c[:, MLA_Q_RANK + MLA_KV_RANK:], rope_dim)
    w_ukv = p['mla_w_ukv'][l].reshape(MLA_KV_RANK, MLA_HEADS, MLA_NOPE + MLA_V)
    w_uq = p['mla_w_uq'][l].reshape(MLA_Q_RANK, MLA_HEADS, MLA_DQK)
    per_head = lambda f: jnp.concatenate([f(hd) for hd in range(MLA_HEADS)], axis=1)
    d_src, _, _ = _dil_lane_map()
    w_d = w_in[:, o_d:o_g]
    n_qk = N_DIL * DIL_W
    qk_cols = np.concatenate([which * n_qk + g * DIL_W + d_src for which in range(2) for g in range(N_DIL)])
    n_chunks = D_FF // FFN_CHUNK
    w_up = p['ffn_w_up'][l]
    chunk_cols = lambda w: w.reshape(w.shape[0], n_chunks, FFN_CHUNK).transpose(1, 0, 2)
    cw, cb = p['ffn_conv_w'][l], p['ffn_conv_b'][l].reshape(1, -1)
    return {
        'g_attn': row(p['attn_norm'][l]),
        'w_ab': w_in[:, :o_c].astype(BF16),
        'w_c': jnp.concatenate([w_c[:, :MLA_Q_RANK + MLA_KV_RANK], w_kr], axis=1).astype(BF16),
        'w_d': jnp.concatenate([w_d[:, qk_cols], w_d[:, 2 * n_qk:]], axis=1).astype(BF16),
        'w_g': w_in[:, o_g:].astype(BF16),
        'conv_w': p['conv_w'][l], 'conv_b': row(p['conv_b'][l]),
        'conv_g': row(p['conv_ln_g'][l]), 'conv_beta': row(p['conv_ln_b'][l]),
        'sgu_g': row(p['sgu_ln_g'][l]), 'sgu_b': row(p['sgu_ln_b'][l]),
        'ws': p['sgu_w_s'][l].astype(BF16),
        'bs': jnp.repeat(p['sgu_b_s'][l].T, SGU_CH // SGU_GROUPS, axis=1),
        'g_cq': row(p['mla_g_cq'][l]), 'g_ckv': row(p['mla_g_ckv'][l]),
        'w_uq': per_head(lambda hd: _place_lanes(w_uq[:, hd], m_dim)).astype(BF16),
        'w_uk': per_head(lambda hd: _place_lanes(w_ukv[:, hd, :MLA_NOPE], nope_dim)).astype(BF16),
        'w_uv': _pad_heads(w_ukv[:, :, MLA_NOPE:].reshape(MLA_KV_RANK, -1), MLA_HEADS, MLA_V).astype(BF16),
        'gq_mla': _place_lanes(row(p['mla_g_qn'][l]) * (MLA_DQK ** -0.5 * LOG2E), m_dim),
        'gk_mla': _place_lanes(row(p['mla_g_kn'][l]), m_dim),
        'gq_dil': row(p['dil_g_qn'][l])[:, d_src % DIL_HEAD_DIM] * (DIL_HEAD_DIM ** -0.5 * LOG2E),
        'gk_dil': row(p['dil_g_kn'][l])[:, d_src % DIL_HEAD_DIM],
        'wo_a': p['conv_w_o'][l].astype(BF16), 'wo_b': p['sgu_w_o'][l].astype(BF16),
        'wo_c': _pad_heads(p['mla_w_o'][l].T, MLA_HEADS, MLA_V).T.astype(BF16),
        'wo_d': p['dil_w_o'][l].astype(BF16),
        'w_out': p['w_out'][l].astype(BF16),
        'g_ffn': row(p['ffn_norm'][l]),
        'w_up_a': chunk_cols(w_up[:, :D_FF]).astype(BF16), 'w_up_b': chunk_cols(w_up[:, D_FF:]).astype(BF16),
        'cw_a': chunk_cols(cw[:, :D_FF]), 'cw_b': chunk_cols(cw[:, D_FF:]),
        'cb_a': chunk_cols(cb[:, :D_FF]), 'cb_b': chunk_cols(cb[:, D_FF:]),
        'w_down': p['ffn_w_down'][l].astype(BF16),
    }


def _trunk(x, layers):
    S = x.shape[1]
    _, m_freq, m_sign = _mla_lane_map()
    _, d_freq, d_sign = _dil_lane_map()
    tabs = (_rope_tables(S, MLA_ROPE, MLA_THETA, m_freq, m_sign)
            + _rope_tables(S, ROPE_DIMS, ROPE_THETA, d_freq[:LANES], d_sign[:LANES]))
    for lw in layers:
        (ha, yb, qm, km, vm), qd, kd, vd = _inproj(x, tabs, lw)
        om = _mla_attention(qm, km, vm)
        dil = [_dil_group(qd[g], kd[g], vd[g], g) for g in range(N_DIL)]
        x = _merge(x, ha, yb, om, dil, lw)
        x = _ffn(x, lw)
    return x


def kernel(x_prompt, x_sample, attn_norm, w_in, conv_w, conv_b, conv_ln_g, conv_ln_b, conv_w_o, sgu_ln_g, sgu_ln_b, sgu_w_s, sgu_b_s, sgu_w_o, mla_g_cq, mla_g_ckv, mla_w_uq, mla_w_ukv, mla_g_qn, mla_g_kn, mla_w_o, dil_g_qn, dil_g_kn, dil_w_o, w_out, ffn_norm, ffn_w_up, ffn_conv_w, ffn_conv_b, ffn_w_down):
    p = dict(attn_norm=attn_norm, w_in=w_in, conv_w=conv_w, conv_b=conv_b, conv_ln_g=conv_ln_g, conv_ln_b=conv_ln_b,
             conv_w_o=conv_w_o, sgu_ln_g=sgu_ln_g, sgu_ln_b=sgu_ln_b, sgu_w_s=sgu_w_s, sgu_b_s=sgu_b_s,
             sgu_w_o=sgu_w_o, mla_g_cq=mla_g_cq, mla_g_ckv=mla_g_ckv, mla_w_uq=mla_w_uq, mla_w_ukv=mla_w_ukv,
             mla_g_qn=mla_g_qn, mla_g_kn=mla_g_kn, mla_w_o=mla_w_o, dil_g_qn=dil_g_qn, dil_g_kn=dil_g_kn,
             dil_w_o=dil_w_o, w_out=w_out, ffn_norm=ffn_norm, ffn_w_up=ffn_w_up, ffn_conv_w=ffn_conv_w,
             ffn_conv_b=ffn_conv_b, ffn_w_down=ffn_w_down)
    layers = [_prep_layer(p, l) for l in range(DEPTH)]
    return (_trunk(x_prompt, layers), _trunk(x_sample, layers))
```

```python
import functools
import math

import jax
import jax.numpy as jnp
import numpy as np
from jax import lax
from jax.experimental import pallas as pl
from jax.experimental.pallas import tpu as pltpu

F32 = jnp.float32
BF16 = jnp.bfloat16

D_MODEL = 1024
DEPTH = 2
EPS = 1e-6
NEG_INF = -1e30
CONV_CH = 256
CONV_WIDTH = 31
SGU_CH = 256
SGU_GROUPS = 4
SGU_CHUNK = 128
MLA_HEADS = 4
MLA_Q_RANK = 256
MLA_KV_RANK = 128
MLA_NOPE = 64
MLA_ROPE = 32
MLA_V = 64
MLA_THETA = 10000.0
DIL_GROUPS = ((128, 1), (512, 4), (2048, 16))
N_DIL = 3
DIL_HEADS = 4
DIL_HEAD_DIM = 64
ROPE_THETA = 500000.0
ROPE_DIMS = DIL_HEAD_DIM // 4
N_BRANCH = 4
D_FF = 2816
N_A = 2 * CONV_CH
N_B = 2 * SGU_CH
N_C = MLA_Q_RANK + MLA_KV_RANK + MLA_ROPE
N_D = 3 * N_DIL * DIL_HEADS * DIL_HEAD_DIM
DIL_SIDE = 64
MLA_DQK = MLA_NOPE + MLA_ROPE
DIL_W = DIL_HEADS * DIL_HEAD_DIM
LOG2E = math.log2(math.e)
LN2 = math.log(2.0)

LANES = 128
SUBLANES = 8
VMEM_LIMIT = 56 * 1024 * 1024

TM_PROJ = 512
SUB_PROJ = 256
CONV_HALO = 16
MLA_SCORE_ELEMS = 1 << 20
TQ_DIL = 256
DIL_ROWS_PER_STEP = 512
TM_MERGE = 256
MERGE_CHUNK = 1024
TM_FFN = 1024
FFN_HALO = 16
FFN_CHUNK = 256


def _const_spec(shape):
    nd = len(shape)
    return pl.BlockSpec(shape, lambda *_: (0,) * nd, pipeline_mode=pl.Buffered(1))


def _sigmoid(x):
    return 1.0 / (1.0 + jnp.exp(-x))


def _rms_scale(x, n):
    return lax.rsqrt(jnp.sum(x * x, axis=-1, keepdims=True) * (1.0 / n) + EPS)


def _layer_norm(x, g, b):
    mu = jnp.mean(x, axis=-1, keepdims=True)
    xc = x - mu
    var = jnp.mean(xc * xc, axis=-1, keepdims=True)
    return xc * lax.rsqrt(var + EPS) * g + b


def _rope(x, cos, sin):
    return x * cos + pltpu.roll(x, LANES // 2, 1) * sin


def _inproj_kernel(x_ref, mcos_ref, msin_ref, dcos_ref, dsin_ref,
                   g_attn_ref, w_ab_ref, w_c_ref, w_d_ref,
                   sgu_g_ref, sgu_b_ref, ws_ref, bs_ref,
                   g_cq_ref, g_ckv_ref, w_uq_ref, w_uk_ref, w_uv_ref, gq_mla_ref, gk_mla_ref,
                   gq_dil_ref, gk_dil_ref,
                   ha_ref, yb_ref, qm_ref, km_ref, vm_ref, *dil_and_scratch):
    dil_refs, slab_ref = dil_and_scratch[:3 * N_DIL], dil_and_scratch[3 * N_DIL]
    tm = x_ref.shape[0]
    x = x_ref[...]
    h = (x * _rms_scale(x, D_MODEL) * g_attn_ref[...]).astype(BF16)

    dcos, dsin = dcos_ref[...], dsin_ref[...]
    lane_head = (lax.broadcasted_iota(jnp.int32, (tm, LANES), 1) % 64) // 16
    head_masks = [lane_head == hd for hd in range(DIL_HEADS)]
    slab_i = 0
    for which, gain_ref in enumerate((gq_dil_ref, gk_dil_ref, None)):
        for g in range(N_DIL):
            base = (which * N_DIL + g) * DIL_W
            z = jnp.dot(h, w_d_ref[:, base:base + DIL_W], preferred_element_type=F32)
            x0, x1 = z[:, :LANES], z[:, LANES:]
            if gain_ref is not None:
                sq = x0 * x0 + x1 * x1
                ms = jnp.zeros((tm, LANES), F32)
                for hd in range(DIL_HEADS):
                    ms = jnp.where(head_masks[hd],
                                   jnp.sum(jnp.where(head_masks[hd], sq, 0.0), axis=-1, keepdims=True), ms)
                scale = lax.rsqrt(ms * (1.0 / DIL_HEAD_DIM) + EPS)
                x0 = _rope(x0 * scale * gain_ref[:, :LANES], dcos, dsin)
                x1 = x1 * scale * gain_ref[:, LANES:]
            out_ref = dil_refs[which * N_DIL + g]
            d = out_ref.shape[0]
            for part, xc in enumerate((x0, x1)):
                lanes = slice(part * LANES, (part + 1) * LANES)
                if d == 1:
                    out_ref[0, :, lanes] = xc.astype(BF16)
                else:
                    slab = slab_ref.at[slab_i]
                    slab_i += 1
                    slab[...] = xc
                    for r in range(d):
                        out_ref[r, :, lanes] = slab[pl.ds(r, tm // d, stride=d), :].astype(BF16)

    z_c = jnp.dot(h, w_c_ref[...], preferred_element_type=F32)
    c_q = z_c[:, :MLA_Q_RANK]
    c_q = (c_q * _rms_scale(c_q, MLA_Q_RANK) * g_cq_ref[...]).astype(BF16)
    c_kv = z_c[:, MLA_Q_RANK:MLA_Q_RANK + MLA_KV_RANK]
    c_kv = (c_kv * _rms_scale(c_kv, MLA_KV_RANK) * g_ckv_ref[...]).astype(BF16)
    k_rope = z_c[:, MLA_Q_RANK + MLA_KV_RANK:]
    q = jnp.dot(c_q, w_uq_ref[...], preferred_element_type=F32)
    kn = jnp.dot(c_kv, w_uk_ref[...], preferred_element_type=F32)
    v_lane = lax.broadcasted_iota(jnp.int32, (tm, MLA_HEADS * LANES), 1) % LANES
    vm_ref[...] = jnp.where(v_lane == MLA_V, 1.0,
                            jnp.dot(c_kv, w_uv_ref[...], preferred_element_type=F32)).astype(BF16)
    mcos, msin = mcos_ref[...], msin_ref[...]
    for hd in range(MLA_HEADS):
        cols = slice(hd * LANES, (hd + 1) * LANES)
        qh = q[:, cols]
        qh = qh * _rms_scale(qh, MLA_DQK) * gq_mla_ref[...]
        qm_ref[:, cols] = _rope(qh, mcos, msin).astype(BF16)
        kh = kn[:, cols] + k_rope
        kh = kh * _rms_scale(kh, MLA_DQK) * gk_mla_ref[...]
        km_ref[:, cols] = _rope(kh, mcos, msin).astype(BF16)

    z_a = jnp.dot(h, w_ab_ref[:, :N_A], preferred_element_type=F32)
    ha_ref[...] = z_a[:, :CONV_CH] * _sigmoid(z_a[:, CONV_CH:])
    zb = jnp.dot(h, w_ab_ref[:, N_A:], preferred_element_type=F32)
    zb = 0.5 * zb * (1.0 + jnp.tanh(math.sqrt(2.0 / math.pi) * (zb + 0.044715 * (zb * zb * zb))))
    u = zb[:, :SGU_CH]
    v = _layer_norm(zb[:, SGU_CH:], sgu_g_ref[...], sgu_b_ref[...]).astype(BF16)
    lane = lax.broadcasted_iota(jnp.int32, (SGU_CHUNK, SGU_CH), 1)
    gw = SGU_CH // SGU_GROUPS
    for c in range(tm // SGU_CHUNK):
        rows = slice(c * SGU_CHUNK, (c + 1) * SGU_CHUNK)
        vc = v[rows]
        sv = jnp.dot(ws_ref[SGU_GROUPS - 1], vc, preferred_element_type=F32)
        for g in range(SGU_GROUPS - 2, -1, -1):
            sv = jnp.where(lane < (g + 1) * gw, jnp.dot(ws_ref[g], vc, preferred_element_type=F32), sv)
        yb_ref[rows, :] = (u[rows] * (sv + bs_ref[...])).astype(BF16)


N_PROJ_TOK_OUT = 5
N_PROJ_SLABS = 3 * N_DIL * DIL_W // LANES


def _inproj_tile_kernel(*refs, n_rowwise_in, n_weights):
    rowwise_in = refs[:n_rowwise_in]
    weights = refs[n_rowwise_in:n_rowwise_in + n_weights]
    outs = refs[n_rowwise_in + n_weights:]
    tok_outs, dil_outs, slab_ref = outs[:N_PROJ_TOK_OUT], outs[N_PROJ_TOK_OUT:-1], outs[-1]
    for sub in range(rowwise_in[0].shape[0] // SUB_PROJ):
        rows = pl.ds(sub * SUB_PROJ, SUB_PROJ)
        dil_views = []
        for r in dil_outs:
            per_class = SUB_PROJ // r.shape[0]
            dil_views.append(r.at[:, pl.ds(sub * per_class, per_class)])
        _inproj_kernel(*[r.at[rows] for r in rowwise_in], *weights, *[r.at[rows] for r in tok_outs], *dil_views,
                       slab_ref.at[pl.ds(sub * N_PROJ_SLABS, N_PROJ_SLABS)])


def _inproj(x, tabs, lw):
    B, S, _ = x.shape
    tm = TM_PROJ
    grid = (B, S // tm)
    tok = lambda w, dt: jax.ShapeDtypeStruct((B, S, w), dt)
    tok_spec = lambda w: pl.BlockSpec((None, tm, w), lambda b, i: (b, i, 0))
    tab_spec = pl.BlockSpec((tm, LANES), lambda b, i: (i, 0))
    weights = (lw['g_attn'], lw['w_ab'], lw['w_c'], lw['w_d'], lw['sgu_g'], lw['sgu_b'], lw['ws'], lw['bs'],
               lw['g_cq'], lw['g_ckv'], lw['w_uq'], lw['w_uk'], lw['w_uv'], lw['gq_mla'], lw['gk_mla'],
               lw['gq_dil'], lw['gk_dil'])
    dils = [d for _ in range(3) for (_, d) in DIL_GROUPS]
    outs = pl.pallas_call(
        functools.partial(_inproj_tile_kernel, n_rowwise_in=1 + len(tabs), n_weights=len(weights)),
        grid=grid,
        in_specs=[tok_spec(D_MODEL)] + [tab_spec] * len(tabs) + [_const_spec(w.shape) for w in weights],
        out_specs=[tok_spec(CONV_CH), tok_spec(SGU_CH), tok_spec(4 * LANES), tok_spec(4 * LANES), tok_spec(4 * LANES)]
                  + [pl.BlockSpec((None, d, tm // d, DIL_W), lambda b, i: (b, 0, i, 0)) for d in dils],
        out_shape=[tok(CONV_CH, F32), tok(SGU_CH, BF16), tok(4 * LANES, BF16), tok(4 * LANES, BF16), tok(4 * LANES, BF16)]
                  + [jax.ShapeDtypeStruct((B, d, S // d, DIL_W), BF16) for d in dils],
        scratch_shapes=[pltpu.VMEM((tm // SUB_PROJ * N_PROJ_SLABS, SUB_PROJ, LANES), F32)],
        compiler_params=pltpu.CompilerParams(dimension_semantics=("parallel", "parallel"),
                                             vmem_limit_bytes=VMEM_LIMIT),
        name="inproj",
    )(x, *tabs, *weights)
    return outs[:5], outs[5:5 + N_DIL], outs[5 + N_DIL:5 + 2 * N_DIL], outs[5 + 2 * N_DIL:]


def _conv_module_tile(prev_ref, cur_ref, next_ref, w_ref, b_ref, g_ref, beta_ref):
    tm = cur_ref.shape[0]
    i = pl.program_id(1)
    last = pl.num_programs(1) - 1
    ext = jnp.concatenate([jnp.where(i > 0, prev_ref[...], 0.0), cur_ref[...],
                           jnp.where(i < last, next_ref[...], 0.0)], axis=0)
    n_ext = ext.shape[0]
    shifted = [ext] + [pltpu.roll(ext, n_ext - s, 0) for s in range(1, SUBLANES)]
    acc = jnp.zeros((tm, CONV_CH), F32) + b_ref[...]
    base = CONV_HALO - CONV_WIDTH // 2
    for k in range(CONV_WIDTH):
        a, s = divmod(base + k, SUBLANES)
        acc = acc + shifted[s][a * SUBLANES:a * SUBLANES + tm, :] * w_ref[k:k + 1, :]
    y = _layer_norm(acc, g_ref[...], beta_ref[...])
    return (y * _sigmoid(y)).astype(BF16)


def _halo_specs(tm, halo, width, seq):
    nh = tm // halo
    n_halo_blocks = seq // halo
    return [pl.BlockSpec((None, halo, width), lambda b, i: (b, jnp.maximum(i * nh - 1, 0), 0)),
            pl.BlockSpec((None, tm, width), lambda b, i: (b, i, 0)),
            pl.BlockSpec((None, halo, width), lambda b, i: (b, jnp.minimum((i + 1) * nh, n_halo_blocks - 1), 0))]


def _mla_kernel(q_ref, k_ref, v_ref, o_ref):
    for hd in range(MLA_HEADS):
        cols = slice(hd * LANES, (hd + 1) * LANES)
        s = lax.dot_general(q_ref[:, cols], k_ref[:, cols], (((1,), (1,)), ((), ())), preferred_element_type=F32)
        p = jnp.exp2(s - jnp.max(s, axis=-1, keepdims=True))
        o = jnp.dot(p.astype(BF16), v_ref[:, cols], preferred_element_type=F32)
        l = o[:, MLA_V:MLA_V + 1]
        o_ref[:, cols] = (o / l).astype(BF16)


def _mla_attention(q, k, v):
    B, S, W = q.shape
    tq = min(S, max(LANES, MLA_SCORE_ELEMS // S))
    return pl.pallas_call(
        _mla_kernel,
        grid=(B, S // tq),
        in_specs=[pl.BlockSpec((None, tq, W), lambda b, i: (b, i, 0)),
                  pl.BlockSpec((None, S, W), lambda b, i: (b, 0, 0)),
                  pl.BlockSpec((None, S, W), lambda b, i: (b, 0, 0))],
        out_specs=pl.BlockSpec((None, tq, W), lambda b, i: (b, i, 0)),
        out_shape=jax.ShapeDtypeStruct((B, S, W), BF16),
        compiler_params=pltpu.CompilerParams(dimension_semantics=("parallel", "arbitrary"),
                                             vmem_limit_bytes=VMEM_LIMIT),
        name="mla_attention",
    )(q, k, v)


def _dil_kernel(q_ref, k_ref, v_ref, o_ref, lse_ref, *, tq, tk):
    rb, tqb, _ = q_ref.shape
    n = k_ref.shape[1]
    i_base = pl.program_id(2) * tqb
    lane = lax.broadcasted_iota(jnp.int32, (tq, DIL_W), 1)
    q_head = (lane % 64) // 16
    low_half = lax.broadcasted_iota(jnp.int32, (tq, LANES), 1) < DIL_HEAD_DIM
    for r in range(rb):
        for t in range(tqb // tq):
            i0 = i_base + t * tq
            rows = slice(t * tq, (t + 1) * tq)
            if tk == n:
                start = 0
                kw, vw = k_ref[r], v_ref[r]
            else:
                start = pl.multiple_of(jnp.clip(i0 - DIL_SIDE, 0, n - tk), DIL_SIDE)
                kw, vw = k_ref[r, pl.ds(start, tk), :], v_ref[r, pl.ds(start, tk), :]
            q = q_ref[r, rows, :]
            row = i0 + lax.broadcasted_iota(jnp.int32, (tq, tk), 0)
            col = start + lax.broadcasted_iota(jnp.int32, (tq, tk), 1)
            valid = jnp.abs(row - col) <= DIL_SIDE
            outs, lses = [], []
            for hd in range(DIL_HEADS):
                qh = jnp.where(q_head == hd, q, jnp.zeros_like(q))
                s = lax.dot_general(qh, kw, (((1,), (1,)), ((), ())), preferred_element_type=F32)
                s = jnp.where(valid, s, NEG_INF)
                m = jnp.max(s, axis=-1, keepdims=True)
                p = jnp.exp2(s - m)
                l = jnp.sum(p, axis=-1, keepdims=True)
                col = slice((hd // 2) * LANES, (hd // 2 + 1) * LANES)
                o = jnp.dot(p.astype(BF16), vw, preferred_element_type=F32)
                outs.append(o[:, col] / l)
                lses.append(m * LN2 + jnp.log(l))
            for c in range(DIL_W // LANES):
                col = slice(c * LANES, (c + 1) * LANES)
                o_ref[r, rows, col] = jnp.where(low_half, outs[2 * c], outs[2 * c + 1])
                lse_ref[r, rows, col] = jnp.where(low_half, lses[2 * c], lses[2 * c + 1])


def _dil_group(q, k, v, g):
    B, d, n, _ = q.shape
    tq = min(TQ_DIL, n)
    tk = min(n, tq + 2 * DIL_SIDE)
    tqb = min(DIL_ROWS_PER_STEP, n)
    rb = min(d, max(1, DIL_ROWS_PER_STEP // tqb))
    q_spec = pl.BlockSpec((None, rb, tqb, DIL_W), lambda b, r, i: (b, r, i, 0))
    kv_spec = pl.BlockSpec((None, rb, n, DIL_W), lambda b, r, i: (b, r, 0, 0))
    return pl.pallas_call(
        functools.partial(_dil_kernel, tq=tq, tk=tk),
        grid=(B, d // rb, n // tqb),
        in_specs=[q_spec, kv_spec, kv_spec],
        out_specs=[q_spec, q_spec],
        out_shape=[jax.ShapeDtypeStruct((B, d, n, DIL_W), F32)] * 2,
        compiler_params=pltpu.CompilerParams(dimension_semantics=("parallel", "parallel", "arbitrary"),
                                             vmem_limit_bytes=VMEM_LIMIT),
        name=f"dil_attention_g{g}",
    )(q, k, v)


def _merge_kernel(x_ref, ha_prev_ref, ha_ref, ha_next_ref, yb_ref, om_ref,
                  o0_ref, l0_ref, o1_ref, l1_ref, o2_ref, l2_ref,
                  conv_w_ref, conv_b_ref, conv_g_ref, conv_beta_ref,
                  g_attn_ref, w_g_ref, woa_ref, wob_ref, woc_ref, wod_ref, w_out_ref, y_ref,
                  slab_ref, merged_ref):
    x = x_ref[...]
    tm = x.shape[0]
    h = (x * _rms_scale(x, D_MODEL) * g_attn_ref[...]).astype(BF16)

    def natural(ref, base):
        d = ref.shape[0]
        if d == 1:
            return ref[0]
        parts = []
        for part in range(DIL_W // LANES):
            slab = slab_ref.at[base + part]
            for r in range(d):
                slab[pl.ds(r, tm // d, stride=d), :] = ref[r, :, part * LANES:(part + 1) * LANES]
            parts.append(slab[...])
        return jnp.concatenate(parts, axis=1)

    o0, l0 = natural(o0_ref, 0), natural(l0_ref, 2)
    o1, l1 = natural(o1_ref, 4), natural(l1_ref, 6)
    o2, l2 = natural(o2_ref, 8), natural(l2_ref, 10)
    m = jnp.maximum(jnp.maximum(l0, l1), l2)
    e0, e1, e2 = jnp.exp(l0 - m), jnp.exp(l1 - m), jnp.exp(l2 - m)
    od = ((e0 * o0 + e1 * o1 + e2 * o2) / (e0 + e1 + e2)).astype(BF16)
    ca = _conv_module_tile(ha_prev_ref, ha_ref, ha_next_ref, conv_w_ref, conv_b_ref, conv_g_ref, conv_beta_ref)
    branches = ((ca, woa_ref), (yb_ref[...], wob_ref), (om_ref[...], woc_ref), (od, wod_ref))
    for c in range(D_MODEL // MERGE_CHUNK):
        cols = slice(c * MERGE_CHUNK, (c + 1) * MERGE_CHUNK)
        part = jnp.zeros((tm, MERGE_CHUNK), F32)
        for i, (act, wo_ref) in enumerate(branches):
            gcols = slice(i * D_MODEL + c * MERGE_CHUNK, i * D_MODEL + (c + 1) * MERGE_CHUNK)
            gate = _sigmoid(jnp.dot(h, w_g_ref[:, gcols], preferred_element_type=F32))
            part = part + gate * jnp.dot(act, wo_ref[:, cols], preferred_element_type=F32)
        merged_ref[:, cols] = part.astype(BF16)
    y_ref[...] = x + jnp.dot(merged_ref[...], w_out_ref[...], preferred_element_type=F32)


def _merge(x, ha, yb, om, dil, lw):
    B, S, _ = x.shape
    tm = TM_MERGE
    tok_spec = lambda w: pl.BlockSpec((None, tm, w), lambda b, i: (b, i, 0))
    dil_acts = tuple(a for pair in dil for a in pair)
    dil_spec = lambda d: pl.BlockSpec((None, d, tm // d, DIL_W), lambda b, i: (b, 0, i, 0))
    weights = (lw['conv_w'], lw['conv_b'], lw['conv_g'], lw['conv_beta'],
               lw['g_attn'], lw['w_g'], lw['wo_a'], lw['wo_b'], lw['wo_c'], lw['wo_d'], lw['w_out'])
    return pl.pallas_call(
        _merge_kernel,
        grid=(B, S // tm),
        in_specs=[tok_spec(D_MODEL)] + _halo_specs(tm, CONV_HALO, CONV_CH, S) + [tok_spec(SGU_CH), tok_spec(om.shape[-1])]
                 + [dil_spec(a.shape[1]) for a in dil_acts] + [_const_spec(w.shape) for w in weights],
        out_specs=tok_spec(D_MODEL),
        out_shape=jax.ShapeDtypeStruct((B, S, D_MODEL), F32),
        scratch_shapes=[pltpu.VMEM((2 * N_DIL * DIL_W // LANES, tm, LANES), F32),
                        pltpu.VMEM((tm, D_MODEL), BF16)],
        compiler_params=pltpu.CompilerParams(dimension_semantics=("parallel", "parallel"),
                                             vmem_limit_bytes=VMEM_LIMIT),
        name="merge",
    )(x, ha, ha, ha, yb, om, *dil_acts, *weights)


def _ffn_kernel(prev_ref, cur_ref, next_ref, g_ref, w_up_ref, cw_ref, cb_ref, wd_ref,
                y_ref, h_ref, act_ref):
    tm = cur_ref.shape[0]
    i = pl.program_id(1)
    last = pl.num_programs(1) - 1

    def normed(x):
        return (x * _rms_scale(x, D_MODEL) * g_ref[...]).astype(BF16)

    h_ref[0:FFN_HALO, :] = jnp.where(i > 0, normed(prev_ref[...]), jnp.zeros((FFN_HALO, D_MODEL), BF16))
    h_ref[FFN_HALO:FFN_HALO + tm, :] = normed(cur_ref[...])
    h_ref[FFN_HALO + tm:, :] = jnp.where(i < last, normed(next_ref[...]), jnp.zeros((FFN_HALO, D_MODEL), BF16))

    rows = slice(FFN_HALO, FFN_HALO + tm)
    n_ext = tm + 2 * FFN_HALO

    for j in range(w_up_ref.shape[0]):
        u = jnp.dot(h_ref[...], w_up_ref[j], preferred_element_type=F32)
        cw = cw_ref[j]
        c = (pltpu.roll(u, 1, 0)[rows] * cw[0:1, :] + u[rows] * cw[1:2, :]
             + pltpu.roll(u, n_ext - 1, 0)[rows] * cw[2:3, :] + cb_ref[j])
        a, b = c[:, :FFN_CHUNK], c[:, FFN_CHUNK:]
        act_ref[:, j * FFN_CHUNK:(j + 1) * FFN_CHUNK] = (a * _sigmoid(a) * b).astype(BF16)
    y_ref[...] = cur_ref[...] + jnp.dot(act_ref[...], wd_ref[...], preferred_element_type=F32)


def _ffn(x, lw):
    B, S, _ = x.shape
    tm = min(TM_FFN, S)
    weights = (lw['g_ffn'], lw['w_up'], lw['cw'], lw['cb'], lw['w_down'])
    return pl.pallas_call(
        _ffn_kernel,
        grid=(B, S // tm),
        in_specs=_halo_specs(tm, FFN_HALO, D_MODEL, S) + [_const_spec(w.shape) for w in weights],
        out_specs=pl.BlockSpec((None, tm, D_MODEL), lambda b, i: (b, i, 0)),
        out_shape=jax.ShapeDtypeStruct((B, S, D_MODEL), F32),
        scratch_shapes=[pltpu.VMEM((tm + 2 * FFN_HALO, D_MODEL), BF16),
                        pltpu.VMEM((tm, D_FF), BF16)],
        compiler_params=pltpu.CompilerParams(dimension_semantics=("parallel", "parallel"),
                                             vmem_limit_bytes=VMEM_LIMIT),
        name="conv_ffn",
    )(x, x, x, *weights)


def _mla_lane_map():
    half = MLA_ROPE // 2
    dim = -np.ones(LANES, np.int64)
    freq = -np.ones(LANES, np.int64)
    sign = np.zeros(LANES, np.float32)
    nope_lanes = list(range(half, LANES // 2)) + list(range(LANES // 2 + half, MLA_DQK))
    dim[nope_lanes] = np.arange(MLA_NOPE)
    for i in range(half):
        dim[i], freq[i], sign[i] = MLA_NOPE + i, i, -1.0
        dim[LANES // 2 + i], freq[LANES // 2 + i], sign[LANES // 2 + i] = MLA_NOPE + half + i, i, 1.0
    return dim, freq, sign


def _dil_lane_map():
    half = ROPE_DIMS // 2
    src = np.zeros(DIL_W, np.int64)
    freq = -np.ones(DIL_W, np.int64)
    sign = np.zeros(DIL_W, np.float32)
    for lane in range(DIL_W):
        col, l = divmod(lane, LANES)
        blk, hd, i = l // 64, (l % 64) // 16, l % 16
        if col == 0 and i < half:
            d = blk * half + i
            freq[lane], sign[lane] = i, (-1.0 if blk == 0 else 1.0)
        elif col == 0:
            d = ROPE_DIMS + blk * half + (i - half)
        else:
            d = 2 * ROPE_DIMS + blk * 16 + i
        src[lane] = hd * DIL_HEAD_DIM + d
    return src, freq, sign


def _rope_tables(seq, dims, theta, freq, sign):
    inv = jnp.exp(-math.log(theta) * jnp.arange(0, dims, 2, dtype=F32) / dims)
    ang = jnp.arange(seq, dtype=F32)[:, None] * inv[None, :]
    cos, sin = jnp.cos(ang), jnp.sin(ang)
    on = jnp.asarray(freq >= 0)[None, :]
    idx = np.maximum(freq, 0)
    return (jnp.where(on, cos[:, idx], 1.0), jnp.where(on, sin[:, idx] * jnp.asarray(sign)[None, :], 0.0))


def _place_lanes(w, dim):
    return jnp.where(jnp.asarray(dim >= 0)[None, :], w[:, np.maximum(dim, 0)], 0.0)


def _pad_heads(w, n_heads, width):
    k = w.shape[0]
    w = w.reshape(k, n_heads, width)
    return jnp.pad(w, ((0, 0), (0, 0), (0, LANES - width))).reshape(k, n_heads * LANES)


def _prep_layer(p, l):
    row = lambda a: a.reshape(1, -1).astype(F32)
    w_in = p['w_in'][l]
    o_c = N_A + N_B
    o_d = o_c + N_C
    o_g = o_d + N_D
    w_c = w_in[:, o_c:o_d]
    m_dim, _, _ = _mla_lane_map()
    nope_dim = np.where(m_dim < MLA_NOPE, m_dim, -1)
    rope_dim = np.where(m_dim >= MLA_NOPE, m_dim - MLA_NOPE, -1)
    w_kr = _place_lanes(w_c[:, MLA_Q_RANK + MLA_KV_RANK:], rope_dim)
    w_ukv = p['mla_w_ukv'][l].reshape(MLA_KV_RANK, MLA_HEADS, MLA_NOPE + MLA_V)
    w_uq = p['mla_w_uq'][l].reshape(MLA_Q_RANK, MLA_HEADS, MLA_DQK)
    per_head = lambda f: jnp.concatenate([f(hd) for hd in range(MLA_HEADS)], axis=1)
    d_src, _, _ = _dil_lane_map()
    w_d = w_in[:, o_d:o_g]
    n_qk = N_DIL * DIL_W
    qk_cols = np.concatenate([which * n_qk + g * DIL_W + d_src for which in range(2) for g in range(N_DIL)])
    n_chunks = D_FF // FFN_CHUNK
    w_up = p['ffn_w_up'][l]
    chunk_cols = lambda w: w.reshape(w.shape[0], n_chunks, FFN_CHUNK).transpose(1, 0, 2)
    pair_chunks = lambda w: jnp.concatenate([chunk_cols(w[:, :D_FF]), chunk_cols(w[:, D_FF:])], axis=2)
    cw, cb = p['ffn_conv_w'][l], p['ffn_conv_b'][l].reshape(1, -1)
    return {
        'g_attn': row(p['attn_norm'][l]),
        'w_ab': w_in[:, :o_c].astype(BF16),
        'w_c': jnp.concatenate([w_c[:, :MLA_Q_RANK + MLA_KV_RANK], w_kr], axis=1).astype(BF16),
        'w_d': jnp.concatenate([w_d[:, qk_cols], w_d[:, 2 * n_qk:]], axis=1).astype(BF16),
        'w_g': w_in[:, o_g:].astype(BF16),
        'conv_w': p['conv_w'][l], 'conv_b': row(p['conv_b'][l]),
        'conv_g': row(p['conv_ln_g'][l]), 'conv_beta': row(p['conv_ln_b'][l]),
        'sgu_g': row(p['sgu_ln_g'][l]), 'sgu_b': row(p['sgu_ln_b'][l]),
        'ws': p['sgu_w_s'][l].astype(BF16),
        'bs': jnp.repeat(p['sgu_b_s'][l].T, SGU_CH // SGU_GROUPS, axis=1),
        'g_cq': row(p['mla_g_cq'][l]), 'g_ckv': row(p['mla_g_ckv'][l]),
        'w_uq': per_head(lambda hd: _place_lanes(w_uq[:, hd], m_dim)).astype(BF16),
        'w_uk': per_head(lambda hd: _place_lanes(w_ukv[:, hd, :MLA_NOPE], nope_dim)).astype(BF16),
        'w_uv': _pad_heads(w_ukv[:, :, MLA_NOPE:].reshape(MLA_KV_RANK, -1), MLA_HEADS, MLA_V).astype(BF16),
        'gq_mla': _place_lanes(row(p['mla_g_qn'][l]) * (MLA_DQK ** -0.5 * LOG2E), m_dim),
        'gk_mla': _place_lanes(row(p['mla_g_kn'][l]), m_dim),
        'gq_dil': row(p['dil_g_qn'][l])[:, d_src % DIL_HEAD_DIM] * (DIL_HEAD_DIM ** -0.5 * LOG2E),
        'gk_dil': row(p['dil_g_kn'][l])[:, d_src % DIL_HEAD_DIM],
        'wo_a': p['conv_w_o'][l].astype(BF16), 'wo_b': p['sgu_w_o'][l].astype(BF16),
        'wo_c': _pad_heads(p['mla_w_o'][l].T, MLA_HEADS, MLA_V).T.astype(BF16),
        'wo_d': p['dil_w_o'][l].astype(BF16),
        'w_out': p['w_out'][l].astype(BF16),
        'g_ffn': row(p['ffn_norm'][l]),
        'w_up': pair_chunks(w_up).astype(BF16), 'cw': pair_chunks(cw), 'cb': pair_chunks(cb),
        'w_down': p['ffn_w_down'][l].astype(BF16),
    }


def _trunk(x, layers):
    S = x.shape[1]
    _, m_freq, m_sign = _mla_lane_map()
    _, d_freq, d_sign = _dil_lane_map()
    tabs = (_rope_tables(S, MLA_ROPE, MLA_THETA, m_freq, m_sign)
            + _rope_tables(S, ROPE_DIMS, ROPE_THETA, d_freq[:LANES], d_sign[:LANES]))
    for lw in layers:
        (ha, yb, qm, km, vm), qd, kd, vd = _inproj(x, tabs, lw)
        om = _mla_attention(qm, km, vm)
        dil = [_dil_group(qd[g], kd[g], vd[g], g) for g in range(N_DIL)]
        x = _merge(x, ha, yb, om, dil, lw)
        x = _ffn(x, lw)
    return x


def kernel(x_prompt, x_sample, attn_norm, w_in, conv_w, conv_b, conv_ln_g, conv_ln_b, conv_w_o, sgu_ln_g, sgu_ln_b, sgu_w_s, sgu_b_s, sgu_w_o, mla_g_cq, mla_g_ckv, mla_w_uq, mla_w_ukv, mla_g_qn, mla_g_kn, mla_w_o, dil_g_qn, dil_g_kn, dil_w_o, w_out, ffn_norm, ffn_w_up, ffn_conv_w, ffn_conv_b, ffn_w_down):
    p = dict(attn_norm=attn_norm, w_in=w_in, conv_w=conv_w, conv_b=conv_b, conv_ln_g=conv_ln_g, conv_ln_b=conv_ln_b,
             conv_w_o=conv_w_o, sgu_ln_g=sgu_ln_g, sgu_ln_b=sgu_ln_b, sgu_w_s=sgu_w_s, sgu_b_s=sgu_b_s,
             sgu_w_o=sgu_w_o, mla_g_cq=mla_g_cq, mla_g_ckv=mla_g_ckv, mla_w_uq=mla_w_uq, mla_w_ukv=mla_w_ukv,
             mla_g_qn=mla_g_qn, mla_g_kn=mla_g_kn, mla_w_o=mla_w_o, dil_g_qn=dil_g_qn, dil_g_kn=dil_g_kn,
             dil_w_o=dil_w_o, w_out=w_out, ffn_norm=ffn_norm, ffn_w_up=ffn_w_up, ffn_conv_w=ffn_conv_w,
             ffn_conv_b=ffn_conv_b, ffn_w_down=ffn_w_down)
    layers = [_prep_layer(p, l) for l in range(DEPTH)]
    return (_trunk(x_prompt, layers), _trunk(x_sample, layers))
```

```python
import functools
import math

import jax
import jax.numpy as jnp
import numpy as np
from jax import lax
from jax.experimental import pallas as pl
from jax.experimental.pallas import tpu as pltpu

F32 = jnp.float32
BF16 = jnp.bfloat16

D_MODEL = 1024
DEPTH = 2
EPS = 1e-6
NEG_INF = -1e30
CONV_CH = 256
CONV_WIDTH = 31
SGU_CH = 256
SGU_GROUPS = 4
SGU_CHUNK = 128
MLA_HEADS = 4
MLA_Q_RANK = 256
MLA_KV_RANK = 128
MLA_NOPE = 64
MLA_ROPE = 32
MLA_V = 64
MLA_THETA = 10000.0
DIL_GROUPS = ((128, 1), (512, 4), (2048, 16))
N_DIL = 3
DIL_HEADS = 4
DIL_HEAD_DIM = 64
ROPE_THETA = 500000.0
ROPE_DIMS = DIL_HEAD_DIM // 4
N_BRANCH = 4
D_FF = 2816
N_A = 2 * CONV_CH
N_B = 2 * SGU_CH
N_C = MLA_Q_RANK + MLA_KV_RANK + MLA_ROPE
N_D = 3 * N_DIL * DIL_HEADS * DIL_HEAD_DIM
DIL_SIDE = 64
MLA_DQK = MLA_NOPE + MLA_ROPE
DIL_W = DIL_HEADS * DIL_HEAD_DIM
LOG2E = math.log2(math.e)
LN2 = math.log(2.0)

LANES = 128
SUBLANES = 8
VMEM_LIMIT = 56 * 1024 * 1024

TM_PROJ = 512
SUB_PROJ = 256
CONV_HALO = 16
MLA_SCORE_ELEMS = 1 << 20
TQ_DIL = 256
DIL_ROWS_PER_STEP = 2048
TM_MERGE = 256
MERGE_CHUNK = 1024
TM_FFN = 1024
FFN_HALO = 16
FFN_CHUNK = 256


def _const_spec(shape):
    nd = len(shape)
    return pl.BlockSpec(shape, lambda *_: (0,) * nd, pipeline_mode=pl.Buffered(1))


def _sigmoid(x):
    return 1.0 / (1.0 + jnp.exp(-x))


def _rms_scale(x, n):
    return lax.rsqrt(jnp.sum(x * x, axis=-1, keepdims=True) * (1.0 / n) + EPS)


def _layer_norm(x, g, b):
    mu = jnp.mean(x, axis=-1, keepdims=True)
    xc = x - mu
    var = jnp.mean(xc * xc, axis=-1, keepdims=True)
    return xc * lax.rsqrt(var + EPS) * g + b


def _rope(x, cos, sin):
    return x * cos + pltpu.roll(x, LANES // 2, 1) * sin


def _inproj_kernel(x_ref, mcos_ref, msin_ref, dcos_ref, dsin_ref,
                   g_attn_ref, w_ab_ref, w_c_ref, w_d_ref,
                   sgu_g_ref, sgu_b_ref, ws_ref, bs_ref,
                   g_cq_ref, g_ckv_ref, w_uq_ref, w_uk_ref, w_uv_ref, gq_mla_ref, gk_mla_ref,
                   gq_dil_ref, gk_dil_ref,
                   ha_ref, yb_ref, qm_ref, km_ref, vm_ref, *dil_and_scratch):
    dil_refs, slab_ref = dil_and_scratch[:3 * N_DIL], dil_and_scratch[3 * N_DIL]
    tm = x_ref.shape[0]
    x = x_ref[...]
    h = (x * _rms_scale(x, D_MODEL) * g_attn_ref[...]).astype(BF16)

    dcos, dsin = dcos_ref[...], dsin_ref[...]
    lane_head = (lax.broadcasted_iota(jnp.int32, (tm, LANES), 1) % 64) // 16
    head_masks = [lane_head == hd for hd in range(DIL_HEADS)]
    slab_i = 0
    for which, gain_ref in enumerate((gq_dil_ref, gk_dil_ref, None)):
        for g in range(N_DIL):
            base = (which * N_DIL + g) * DIL_W
            z = jnp.dot(h, w_d_ref[:, base:base + DIL_W], preferred_element_type=F32)
            x0, x1 = z[:, :LANES], z[:, LANES:]
            if gain_ref is not None:
                sq = x0 * x0 + x1 * x1
                ms = jnp.zeros((tm, LANES), F32)
                for hd in range(DIL_HEADS):
                    ms = jnp.where(head_masks[hd],
                                   jnp.sum(jnp.where(head_masks[hd], sq, 0.0), axis=-1, keepdims=True), ms)
                scale = lax.rsqrt(ms * (1.0 / DIL_HEAD_DIM) + EPS)
                x0 = _rope(x0 * scale * gain_ref[:, :LANES], dcos, dsin)
                x1 = x1 * scale * gain_ref[:, LANES:]
            out_ref = dil_refs[which * N_DIL + g]
            d = out_ref.shape[0]
            for part, xc in enumerate((x0, x1)):
                lanes = slice(part * LANES, (part + 1) * LANES)
                if d == 1:
                    out_ref[0, :, lanes] = xc.astype(BF16)
                else:
                    slab = slab_ref.at[slab_i]
                    slab_i += 1
                    slab[...] = xc
                    for r in range(d):
                        out_ref[r, :, lanes] = slab[pl.ds(r, tm // d, stride=d), :].astype(BF16)

    z_c = jnp.dot(h, w_c_ref[...], preferred_element_type=F32)
    c_q = z_c[:, :MLA_Q_RANK]
    c_q = (c_q * _rms_scale(c_q, MLA_Q_RANK) * g_cq_ref[...]).astype(BF16)
    c_kv = z_c[:, MLA_Q_RANK:MLA_Q_RANK + MLA_KV_RANK]
    c_kv = (c_kv * _rms_scale(c_kv, MLA_KV_RANK) * g_ckv_ref[...]).astype(BF16)
    k_rope = z_c[:, MLA_Q_RANK + MLA_KV_RANK:]
    q = jnp.dot(c_q, w_uq_ref[...], preferred_element_type=F32)
    kn = jnp.dot(c_kv, w_uk_ref[...], preferred_element_type=F32)
    v_lane = lax.broadcasted_iota(jnp.int32, (tm, MLA_HEADS * LANES), 1) % LANES
    vm_ref[...] = jnp.where(v_lane == MLA_V, 1.0,
                            jnp.dot(c_kv, w_uv_ref[...], preferred_element_type=F32)).astype(BF16)
    mcos, msin = mcos_ref[...], msin_ref[...]
    for hd in range(MLA_HEADS):
        cols = slice(hd * LANES, (hd + 1) * LANES)
        qh = q[:, cols]
        qh = qh * _rms_scale(qh, MLA_DQK) * gq_mla_ref[...]
        qm_ref[:, cols] = _rope(qh, mcos, msin).astype(BF16)
        kh = kn[:, cols] + k_rope
        kh = kh * _rms_scale(kh, MLA_DQK) * gk_mla_ref[...]
        km_ref[:, cols] = _rope(kh, mcos, msin).astype(BF16)

    z_a = jnp.dot(h, w_ab_ref[:, :N_A], preferred_element_type=F32)
    ha_ref[...] = z_a[:, :CONV_CH] * _sigmoid(z_a[:, CONV_CH:])
    zb = jnp.dot(h, w_ab_ref[:, N_A:], preferred_element_type=F32)
    zb = 0.5 * zb * (1.0 + jnp.tanh(math.sqrt(2.0 / math.pi) * (zb + 0.044715 * (zb * zb * zb))))
    u = zb[:, :SGU_CH]
    v = _layer_norm(zb[:, SGU_CH:], sgu_g_ref[...], sgu_b_ref[...]).astype(BF16)
    lane = lax.broadcasted_iota(jnp.int32, (SGU_CHUNK, SGU_CH), 1)
    gw = SGU_CH // SGU_GROUPS
    for c in range(tm // SGU_CHUNK):
        rows = slice(c * SGU_CHUNK, (c + 1) * SGU_CHUNK)
        vc = v[rows]
        sv = jnp.dot(ws_ref[SGU_GROUPS - 1], vc, preferred_element_type=F32)
        for g in range(SGU_GROUPS - 2, -1, -1):
            sv = jnp.where(lane < (g + 1) * gw, jnp.dot(ws_ref[g], vc, preferred_element_type=F32), sv)
        yb_ref[rows, :] = (u[rows] * (sv + bs_ref[...])).astype(BF16)


N_PROJ_TOK_OUT = 5
N_PROJ_SLABS = 3 * N_DIL * DIL_W // LANES


def _inproj_tile_kernel(*refs, n_rowwise_in, n_weights):
    rowwise_in = refs[:n_rowwise_in]
    weights = refs[n_rowwise_in:n_rowwise_in + n_weights]
    outs = refs[n_rowwise_in + n_weights:]
    tok_outs, dil_outs, slab_ref = outs[:N_PROJ_TOK_OUT], outs[N_PROJ_TOK_OUT:-1], outs[-1]
    for sub in range(rowwise_in[0].shape[0] // SUB_PROJ):
        rows = pl.ds(sub * SUB_PROJ, SUB_PROJ)
        dil_views = []
        for r in dil_outs:
            per_class = SUB_PROJ // r.shape[0]
            dil_views.append(r.at[:, pl.ds(sub * per_class, per_class)])
        _inproj_kernel(*[r.at[rows] for r in rowwise_in], *weights, *[r.at[rows] for r in tok_outs], *dil_views,
                       slab_ref.at[pl.ds(sub * N_PROJ_SLABS, N_PROJ_SLABS)])


def _inproj(x, tabs, lw):
    B, S, _ = x.shape
    tm = TM_PROJ
    grid = (B, S // tm)
    tok = lambda w, dt: jax.ShapeDtypeStruct((B, S, w), dt)
    tok_spec = lambda w: pl.BlockSpec((None, tm, w), lambda b, i: (b, i, 0))
    tab_spec = pl.BlockSpec((tm, LANES), lambda b, i: (i, 0))
    weights = (lw['g_attn'], lw['w_ab'], lw['w_c'], lw['w_d'], lw['sgu_g'], lw['sgu_b'], lw['ws'], lw['bs'],
               lw['g_cq'], lw['g_ckv'], lw['w_uq'], lw['w_uk'], lw['w_uv'], lw['gq_mla'], lw['gk_mla'],
               lw['gq_dil'], lw['gk_dil'])
    dils = [d for _ in range(3) for (_, d) in DIL_GROUPS]
    outs = pl.pallas_call(
        functools.partial(_inproj_tile_kernel, n_rowwise_in=1 + len(tabs), n_weights=len(weights)),
        grid=grid,
        in_specs=[tok_spec(D_MODEL)] + [tab_spec] * len(tabs) + [_const_spec(w.shape) for w in weights],
        out_specs=[tok_spec(CONV_CH), tok_spec(SGU_CH), tok_spec(4 * LANES), tok_spec(4 * LANES), tok_spec(4 * LANES)]
                  + [pl.BlockSpec((None, d, tm // d, DIL_W), lambda b, i: (b, 0, i, 0)) for d in dils],
        out_shape=[tok(CONV_CH, F32), tok(SGU_CH, BF16), tok(4 * LANES, BF16), tok(4 * LANES, BF16), tok(4 * LANES, BF16)]
                  + [jax.ShapeDtypeStruct((B, d, S // d, DIL_W), BF16) for d in dils],
        scratch_shapes=[pltpu.VMEM((tm // SUB_PROJ * N_PROJ_SLABS, SUB_PROJ, LANES), F32)],
        compiler_params=pltpu.CompilerParams(dimension_semantics=("parallel", "parallel"),
                                             vmem_limit_bytes=VMEM_LIMIT),
        name="inproj",
    )(x, *tabs, *weights)
    return outs[:5], outs[5:5 + N_DIL], outs[5 + N_DIL:5 + 2 * N_DIL], outs[5 + 2 * N_DIL:]


def _conv_module_tile(prev_ref, cur_ref, next_ref, w_ref, b_ref, g_ref, beta_ref):
    tm = cur_ref.shape[0]
    i = pl.program_id(1)
    last = pl.num_programs(1) - 1
    ext = jnp.concatenate([jnp.where(i > 0, prev_ref[...], 0.0), cur_ref[...],
                           jnp.where(i < last, next_ref[...], 0.0)], axis=0)
    n_ext = ext.shape[0]
    shifted = [ext] + [pltpu.roll(ext, n_ext - s, 0) for s in range(1, SUBLANES)]
    acc = jnp.zeros((tm, CONV_CH), F32) + b_ref[...]
    base = CONV_HALO - CONV_WIDTH // 2
    for k in range(CONV_WIDTH):
        a, s = divmod(base + k, SUBLANES)
        acc = acc + shifted[s][a * SUBLANES:a * SUBLANES + tm, :] * w_ref[k:k + 1, :]
    y = _layer_norm(acc, g_ref[...], beta_ref[...])
    return (y * _sigmoid(y)).astype(BF16)


def _halo_specs(tm, halo, width, seq):
    nh = tm // halo
    n_halo_blocks = seq // halo
    return [pl.BlockSpec((None, halo, width), lambda b, i: (b, jnp.maximum(i * nh - 1, 0), 0)),
            pl.BlockSpec((None, tm, width), lambda b, i: (b, i, 0)),
            pl.BlockSpec((None, halo, width), lambda b, i: (b, jnp.minimum((i + 1) * nh, n_halo_blocks - 1), 0))]


def _mla_kernel(q_ref, k_ref, v_ref, o_ref):
    for hd in range(MLA_HEADS):
        cols = slice(hd * LANES, (hd + 1) * LANES)
        s = lax.dot_general(q_ref[:, cols], k_ref[:, cols], (((1,), (1,)), ((), ())), preferred_element_type=F32)
        p = jnp.exp2(s - jnp.max(s, axis=-1, keepdims=True))
        o = jnp.dot(p.astype(BF16), v_ref[:, cols], preferred_element_type=F32)
        l = o[:, MLA_V:MLA_V + 1]
        o_ref[:, cols] = (o / l).astype(BF16)


def _mla_attention(q, k, v):
    B, S, W = q.shape
    tq = min(S, max(LANES, MLA_SCORE_ELEMS // S))
    return pl.pallas_call(
        _mla_kernel,
        grid=(B, S // tq),
        in_specs=[pl.BlockSpec((None, tq, W), lambda b, i: (b, i, 0)),
                  pl.BlockSpec((None, S, W), lambda b, i: (b, 0, 0)),
                  pl.BlockSpec((None, S, W), lambda b, i: (b, 0, 0))],
        out_specs=pl.BlockSpec((None, tq, W), lambda b, i: (b, i, 0)),
        out_shape=jax.ShapeDtypeStruct((B, S, W), BF16),
        compiler_params=pltpu.CompilerParams(dimension_semantics=("parallel", "arbitrary"),
                                             vmem_limit_bytes=VMEM_LIMIT),
        name="mla_attention",
    )(q, k, v)


def _dil_kernel(q_ref, k_ref, v_ref, o_ref, lse_ref, *, tq, tk):
    rb, tqb, _ = q_ref.shape
    n = k_ref.shape[1]
    i_base = pl.program_id(2) * tqb
    lane = lax.broadcasted_iota(jnp.int32, (tq, DIL_W), 1)
    q_head = (lane % 64) // 16
    low_half = lax.broadcasted_iota(jnp.int32, (tq, LANES), 1) < DIL_HEAD_DIM
    for r in range(rb):
        for t in range(tqb // tq):
            i0 = i_base + t * tq
            rows = slice(t * tq, (t + 1) * tq)
            if tk == n:
                start = 0
                kw, vw = k_ref[r], v_ref[r]
            else:
                start = pl.multiple_of(jnp.clip(i0 - DIL_SIDE, 0, n - tk), DIL_SIDE)
                kw, vw = k_ref[r, pl.ds(start, tk), :], v_ref[r, pl.ds(start, tk), :]
            q = q_ref[r, rows, :]
            row = i0 + lax.broadcasted_iota(jnp.int32, (tq, tk), 0)
            col = start + lax.broadcasted_iota(jnp.int32, (tq, tk), 1)
            valid = jnp.abs(row - col) <= DIL_SIDE
            outs, lses = [], []
            for hd in range(DIL_HEADS):
                qh = jnp.where(q_head == hd, q, jnp.zeros_like(q))
                s = lax.dot_general(qh, kw, (((1,), (1,)), ((), ())), preferred_element_type=F32)
                s = jnp.where(valid, s, NEG_INF)
                m = jnp.max(s, axis=-1, keepdims=True)
                p = jnp.exp2(s - m)
                l = jnp.sum(p, axis=-1, keepdims=True)
                col = slice((hd // 2) * LANES, (hd // 2 + 1) * LANES)
                o = jnp.dot(p.astype(BF16), vw, preferred_element_type=F32)
                outs.append(o[:, col] / l)
                lses.append(m * LN2 + jnp.log(l))
            for c in range(DIL_W // LANES):
                col = slice(c * LANES, (c + 1) * LANES)
                o_ref[r, rows, col] = jnp.where(low_half, outs[2 * c], outs[2 * c + 1])
                lse_ref[r, rows, col] = jnp.where(low_half, lses[2 * c], lses[2 * c + 1])


def _dil_group(q, k, v, g):
    B, d, n, _ = q.shape
    tq = min(TQ_DIL, n)
    tk = min(n, tq + 2 * DIL_SIDE)
    tqb = min(DIL_ROWS_PER_STEP, n)
    rb = min(d, max(1, DIL_ROWS_PER_STEP // tqb))
    q_spec = pl.BlockSpec((None, rb, tqb, DIL_W), lambda b, r, i: (b, r, i, 0))
    kv_spec = pl.BlockSpec((None, rb, n, DIL_W), lambda b, r, i: (b, r, 0, 0))
    return pl.pallas_call(
        functools.partial(_dil_kernel, tq=tq, tk=tk),
        grid=(B, d // rb, n // tqb),
        in_specs=[q_spec, kv_spec, kv_spec],
        out_specs=[q_spec, q_spec],
        out_shape=[jax.ShapeDtypeStruct((B, d, n, DIL_W), F32)] * 2,
        compiler_params=pltpu.CompilerParams(dimension_semantics=("parallel", "parallel", "arbitrary"),
                                             vmem_limit_bytes=VMEM_LIMIT),
        name=f"dil_attention_g{g}",
    )(q, k, v)


def _merge_kernel(x_ref, ha_prev_ref, ha_ref, ha_next_ref, yb_ref, om_ref,
                  o0_ref, l0_ref, o1_ref, l1_ref, o2_ref, l2_ref,
                  conv_w_ref, conv_b_ref, conv_g_ref, conv_beta_ref,
                  g_attn_ref, w_g_ref, woa_ref, wob_ref, woc_ref, wod_ref, w_out_ref, y_ref,
                  slab_ref, merged_ref):
    x = x_ref[...]
    tm = x.shape[0]
    h = (x * _rms_scale(x, D_MODEL) * g_attn_ref[...]).astype(BF16)

    def natural(ref, base):
        d = ref.shape[0]
        if d == 1:
            return ref[0]
        parts = []
        for part in range(DIL_W // LANES):
            slab = slab_ref.at[base + part]
            for r in range(d):
                slab[pl.ds(r, tm // d, stride=d), :] = ref[r, :, part * LANES:(part + 1) * LANES]
            parts.append(slab[...])
        return jnp.concatenate(parts, axis=1)

    o0, l0 = natural(o0_ref, 0), natural(l0_ref, 2)
    o1, l1 = natural(o1_ref, 4), natural(l1_ref, 6)
    o2, l2 = natural(o2_ref, 8), natural(l2_ref, 10)
    m = jnp.maximum(jnp.maximum(l0, l1), l2)
    e0, e1, e2 = jnp.exp(l0 - m), jnp.exp(l1 - m), jnp.exp(l2 - m)
    od = ((e0 * o0 + e1 * o1 + e2 * o2) / (e0 + e1 + e2)).astype(BF16)
    ca = _conv_module_tile(ha_prev_ref, ha_ref, ha_next_ref, conv_w_ref, conv_b_ref, conv_g_ref, conv_beta_ref)
    branches = ((ca, woa_ref), (yb_ref[...], wob_ref), (om_ref[...], woc_ref), (od, wod_ref))
    for c in range(D_MODEL // MERGE_CHUNK):
        cols = slice(c * MERGE_CHUNK, (c + 1) * MERGE_CHUNK)
        part = jnp.zeros((tm, MERGE_CHUNK), F32)
        for i, (act, wo_ref) in enumerate(branches):
            gcols = slice(i * D_MODEL + c * MERGE_CHUNK, i * D_MODEL + (c + 1) * MERGE_CHUNK)
            gate = _sigmoid(jnp.dot(h, w_g_ref[:, gcols], preferred_element_type=F32))
            part = part + gate * jnp.dot(act, wo_ref[:, cols], preferred_element_type=F32)
        merged_ref[:, cols] = part.astype(BF16)
    y_ref[...] = x + jnp.dot(merged_ref[...], w_out_ref[...], preferred_element_type=F32)


def _merge(x, ha, yb, om, dil, lw):
    B, S, _ = x.shape
    tm = TM_MERGE
    tok_spec = lambda w: pl.BlockSpec((None, tm, w), lambda b, i: (b, i, 0))
    dil_acts = tuple(a for pair in dil for a in pair)
    dil_spec = lambda d: pl.BlockSpec((None, d, tm // d, DIL_W), lambda b, i: (b, 0, i, 0))
    weights = (lw['conv_w'], lw['conv_b'], lw['conv_g'], lw['conv_beta'],
               lw['g_attn'], lw['w_g'], lw['wo_a'], lw['wo_b'], lw['wo_c'], lw['wo_d'], lw['w_out'])
    return pl.pallas_call(
        _merge_kernel,
        grid=(B, S // tm),
        in_specs=[tok_spec(D_MODEL)] + _halo_specs(tm, CONV_HALO, CONV_CH, S) + [tok_spec(SGU_CH), tok_spec(om.shape[-1])]
                 + [dil_spec(a.shape[1]) for a in dil_acts] + [_const_spec(w.shape) for w in weights],
        out_specs=tok_spec(D_MODEL),
        out_shape=jax.ShapeDtypeStruct((B, S, D_MODEL), F32),
        scratch_shapes=[pltpu.VMEM((2 * N_DIL * DIL_W // LANES, tm, LANES), F32),
                        pltpu.VMEM((tm, D_MODEL), BF16)],
        compiler_params=pltpu.CompilerParams(dimension_semantics=("parallel", "parallel"),
                                             vmem_limit_bytes=VMEM_LIMIT),
        name="merge",
    )(x, ha, ha, ha, yb, om, *dil_acts, *weights)


def _ffn_kernel(prev_ref, cur_ref, next_ref, g_ref, w_up_ref, cw_ref, cb_ref, wd_ref,
                y_ref, h_ref, act_ref):
    tm = cur_ref.shape[0]
    i = pl.program_id(1)
    last = pl.num_programs(1) - 1

    def normed(x):
        return (x * _rms_scale(x, D_MODEL) * g_ref[...]).astype(BF16)

    h_ref[0:FFN_HALO, :] = jnp.where(i > 0, normed(prev_ref[...]), jnp.zeros((FFN_HALO, D_MODEL), BF16))
    h_ref[FFN_HALO:FFN_HALO + tm, :] = normed(cur_ref[...])
    h_ref[FFN_HALO + tm:, :] = jnp.where(i < last, normed(next_ref[...]), jnp.zeros((FFN_HALO, D_MODEL), BF16))

    rows = slice(FFN_HALO, FFN_HALO + tm)
    n_ext = tm + 2 * FFN_HALO

    for j in range(w_up_ref.shape[0]):
        u = jnp.dot(h_ref[...], w_up_ref[j], preferred_element_type=F32)
        cw = cw_ref[j]
        c = (pltpu.roll(u, 1, 0)[rows] * cw[0:1, :] + u[rows] * cw[1:2, :]
             + pltpu.roll(u, n_ext - 1, 0)[rows] * cw[2:3, :] + cb_ref[j])
        a, b = c[:, :FFN_CHUNK], c[:, FFN_CHUNK:]
        act_ref[:, j * FFN_CHUNK:(j + 1) * FFN_CHUNK] = (a * _sigmoid(a) * b).astype(BF16)
    y_ref[...] = cur_ref[...] + jnp.dot(act_ref[...], wd_ref[...], preferred_element_type=F32)


def _ffn(x, lw):
    B, S, _ = x.shape
    tm = min(TM_FFN, S)
    weights = (lw['g_ffn'], lw['w_up'], lw['cw'], lw['cb'], lw['w_down'])
    return pl.pallas_call(
        _ffn_kernel,
        grid=(B, S // tm),
        in_specs=_halo_specs(tm, FFN_HALO, D_MODEL, S) + [_const_spec(w.shape) for w in weights],
        out_specs=pl.BlockSpec((None, tm, D_MODEL), lambda b, i: (b, i, 0)),
        out_shape=jax.ShapeDtypeStruct((B, S, D_MODEL), F32),
        scratch_shapes=[pltpu.VMEM((tm + 2 * FFN_HALO, D_MODEL), BF16),
                        pltpu.VMEM((tm, D_FF), BF16)],
        compiler_params=pltpu.CompilerParams(dimension_semantics=("parallel", "parallel"),
                                             vmem_limit_bytes=VMEM_LIMIT),
        name="conv_ffn",
    )(x, x, x, *weights)


def _mla_lane_map():
    half = MLA_ROPE // 2
    dim = -np.ones(LANES, np.int64)
    freq = -np.ones(LANES, np.int64)
    sign = np.zeros(LANES, np.float32)
    nope_lanes = list(range(half, LANES // 2)) + list(range(LANES // 2 + half, MLA_DQK))
    dim[nope_lanes] = np.arange(MLA_NOPE)
    for i in range(half):
        dim[i], freq[i], sign[i] = MLA_NOPE + i, i, -1.0
        dim[LANES // 2 + i], freq[LANES // 2 + i], sign[LANES // 2 + i] = MLA_NOPE + half + i, i, 1.0
    return dim, freq, sign


def _dil_lane_map():
    half = ROPE_DIMS // 2
    src = np.zeros(DIL_W, np.int64)
    freq = -np.ones(DIL_W, np.int64)
    sign = np.zeros(DIL_W, np.float32)
    for lane in range(DIL_W):
        col, l = divmod(lane, LANES)
        blk, hd, i = l // 64, (l % 64) // 16, l % 16
        if col == 0 and i < half:
            d = blk * half + i
            freq[lane], sign[lane] = i, (-1.0 if blk == 0 else 1.0)
        elif col == 0:
            d = ROPE_DIMS + blk * half + (i - half)
        else:
            d = 2 * ROPE_DIMS + blk * 16 + i
        src[lane] = hd * DIL_HEAD_DIM + d
    return src, freq, sign


def _rope_tables(seq, dims, theta, freq, sign):
    inv = jnp.exp(-math.log(theta) * jnp.arange(0, dims, 2, dtype=F32) / dims)
    ang = jnp.arange(seq, dtype=F32)[:, None] * inv[None, :]
    cos, sin = jnp.cos(ang), jnp.sin(ang)
    on = jnp.asarray(freq >= 0)[None, :]
    idx = np.maximum(freq, 0)
    return (jnp.where(on, cos[:, idx], 1.0), jnp.where(on, sin[:, idx] * jnp.asarray(sign)[None, :], 0.0))


def _place_lanes(w, dim):
    return jnp.where(jnp.asarray(dim >= 0)[None, :], w[:, np.maximum(dim, 0)], 0.0)


def _pad_heads(w, n_heads, width):
    k = w.shape[0]
    w = w.reshape(k, n_heads, width)
    return jnp.pad(w, ((0, 0), (0, 0), (0, LANES - width))).reshape(k, n_heads * LANES)


def _prep_layer(p, l):
    row = lambda a: a.reshape(1, -1).astype(F32)
    w_in = p['w_in'][l]
    o_c = N_A + N_B
    o_d = o_c + N_C
    o_g = o_d + N_D
    w_c = w_in[:, o_c:o_d]
    m_dim, _, _ = _mla_lane_map()
    nope_dim = np.where(m_dim < MLA_NOPE, m_dim, -1)
    rope_dim = np.where(m_dim >= MLA_NOPE, m_dim - MLA_NOPE, -1)
    w_kr = _place_lanes(w_c[:, MLA_Q_RANK + MLA_KV_RANK:], rope_dim)
    w_ukv = p['mla_w_ukv'][l].reshape(MLA_KV_RANK, MLA_HEADS, MLA_NOPE + MLA_V)
    w_uq = p['mla_w_uq'][l].reshape(MLA_Q_RANK, MLA_HEADS, MLA_DQK)
    per_head = lambda f: jnp.concatenate([f(hd) for hd in range(MLA_HEADS)], axis=1)
    d_src, _, _ = _dil_lane_map()
    w_d = w_in[:, o_d:o_g]
    n_qk = N_DIL * DIL_W
    qk_cols = np.concatenate([which * n_qk + g * DIL_W + d_src for which in range(2) for g in range(N_DIL)])
    n_chunks = D_FF // FFN_CHUNK
    w_up = p['ffn_w_up'][l]
    chunk_cols = lambda w: w.reshape(w.shape[0], n_chunks, FFN_CHUNK).transpose(1, 0, 2)
    pair_chunks = lambda w: jnp.concatenate([chunk_cols(w[:, :D_FF]), chunk_cols(w[:, D_FF:])], axis=2)
    cw, cb = p['ffn_conv_w'][l], p['ffn_conv_b'][l].reshape(1, -1)
    return {
        'g_attn': row(p['attn_norm'][l]),
        'w_ab': w_in[:, :o_c].astype(BF16),
        'w_c': jnp.concatenate([w_c[:, :MLA_Q_RANK + MLA_KV_RANK], w_kr], axis=1).astype(BF16),
        'w_d': jnp.concatenate([w_d[:, qk_cols], w_d[:, 2 * n_qk:]], axis=1).astype(BF16),
        'w_g': w_in[:, o_g:].astype(BF16),
        'conv_w': p['conv_w'][l], 'conv_b': row(p['conv_b'][l]),
        'conv_g': row(p['conv_ln_g'][l]), 'conv_beta': row(p['conv_ln_b'][l]),
        'sgu_g': row(p['sgu_ln_g'][l]), 'sgu_b': row(p['sgu_ln_b'][l]),
        'ws': p['sgu_w_s'][l].astype(BF16),
        'bs': jnp.repeat(p['sgu_b_s'][l].T, SGU_CH // SGU_GROUPS, axis=1),
        'g_cq': row(p['mla_g_cq'][l]), 'g_ckv': row(p['mla_g_ckv'][l]),
        'w_uq': per_head(lambda hd: _place_lanes(w_uq[:, hd], m_dim)).astype(BF16),
        'w_uk': per_head(lambda hd: _place_lanes(w_ukv[:, hd, :MLA_NOPE], nope_dim)).astype(BF16),
        'w_uv': _pad_heads(w_ukv[:, :, MLA_NOPE:].reshape(MLA_KV_RANK, -1), MLA_HEADS, MLA_V).astype(BF16),
        'gq_mla': _place_lanes(row(p['mla_g_qn'][l]) * (MLA_DQK ** -0.5 * LOG2E), m_dim),
        'gk_mla': _place_lanes(row(p['mla_g_kn'][l]), m_dim),
        'gq_dil': row(p['dil_g_qn'][l])[:, d_src % DIL_HEAD_DIM] * (DIL_HEAD_DIM ** -0.5 * LOG2E),
        'gk_dil': row(p['dil_g_kn'][l])[:, d_src % DIL_HEAD_DIM],
        'wo_a': p['conv_w_o'][l].astype(BF16), 'wo_b': p['sgu_w_o'][l].astype(BF16),
        'wo_c': _pad_heads(p['mla_w_o'][l].T, MLA_HEADS, MLA_V).T.astype(BF16),
        'wo_d': p['dil_w_o'][l].astype(BF16),
        'w_out': p['w_out'][l].astype(BF16),
        'g_ffn': row(p['ffn_norm'][l]),
        'w_up': pair_chunks(w_up).astype(BF16), 'cw': pair_chunks(cw), 'cb': pair_chunks(cb),
        'w_down': p['ffn_w_down'][l].astype(BF16),
    }


def _trunk(x, layers):
    S = x.shape[1]
    _, m_freq, m_sign = _mla_lane_map()
    _, d_freq, d_sign = _dil_lane_map()
    tabs = (_rope_tables(S, MLA_ROPE, MLA_THETA, m_freq, m_sign)
            + _rope_tables(S, ROPE_DIMS, ROPE_THETA, d_freq[:LANES], d_sign[:LANES]))
    for lw in layers:
        (ha, yb, qm, km, vm), qd, kd, vd = _inproj(x, tabs, lw)
        om = _mla_attention(qm, km, vm)
        dil = [_dil_group(qd[g], kd[g], vd[g], g) for g in range(N_DIL)]
        x = _merge(x, ha, yb, om, dil, lw)
        x = _ffn(x, lw)
    return x


def kernel(x_prompt, x_sample, attn_norm, w_in, conv_w, conv_b, conv_ln_g, conv_ln_b, conv_w_o, sgu_ln_g, sgu_ln_b, sgu_w_s, sgu_b_s, sgu_w_o, mla_g_cq, mla_g_ckv, mla_w_uq, mla_w_ukv, mla_g_qn, mla_g_kn, mla_w_o, dil_g_qn, dil_g_kn, dil_w_o, w_out, ffn_norm, ffn_w_up, ffn_conv_w, ffn_conv_b, ffn_w_down):
    p = dict(attn_norm=attn_norm, w_in=w_in, conv_w=conv_w, conv_b=conv_b, conv_ln_g=conv_ln_g, conv_ln_b=conv_ln_b,
             conv_w_o=conv_w_o, sgu_ln_g=sgu_ln_g, sgu_ln_b=sgu_ln_b, sgu_w_s=sgu_w_s, sgu_b_s=sgu_b_s,
             sgu_w_o=sgu_w_o, mla_g_cq=mla_g_cq, mla_g_ckv=mla_g_ckv, mla_w_uq=mla_w_uq, mla_w_ukv=mla_w_ukv,
             mla_g_qn=mla_g_qn, mla_g_kn=mla_g_kn, mla_w_o=mla_w_o, dil_g_qn=dil_g_qn, dil_g_kn=dil_g_kn,
             dil_w_o=dil_w_o, w_out=w_out, ffn_norm=ffn_norm, ffn_w_up=ffn_w_up, ffn_conv_w=ffn_conv_w,
             ffn_conv_b=ffn_conv_b, ffn_w_down=ffn_w_down)
    layers = [_prep_layer(p, l) for l in range(DEPTH)]
    return (_trunk(x_prompt, layers), _trunk(x_sample, layers))
```

```python
import functools
import math

import jax
import jax.numpy as jnp
import numpy as np
from jax import lax
from jax.experimental import pallas as pl
from jax.experimental.pallas import tpu as pltpu

F32 = jnp.float32
BF16 = jnp.bfloat16

D_MODEL = 1024
DEPTH = 2
EPS = 1e-6
NEG_INF = -1e30
CONV_CH = 256
CONV_WIDTH = 31
SGU_CH = 256
SGU_GROUPS = 4
SGU_CHUNK = 128
MLA_HEADS = 4
MLA_Q_RANK = 256
MLA_KV_RANK = 128
MLA_NOPE = 64
MLA_ROPE = 32
MLA_V = 64
MLA_THETA = 10000.0
DIL_GROUPS = ((128, 1), (512, 4), (2048, 16))
N_DIL = 3
DIL_HEADS = 4
DIL_HEAD_DIM = 64
ROPE_THETA = 500000.0
ROPE_DIMS = DIL_HEAD_DIM // 4
N_BRANCH = 4
D_FF = 2816
N_A = 2 * CONV_CH
N_B = 2 * SGU_CH
N_C = MLA_Q_RANK + MLA_KV_RANK + MLA_ROPE
N_D = 3 * N_DIL * DIL_HEADS * DIL_HEAD_DIM
DIL_SIDE = 64
MLA_DQK = MLA_NOPE + MLA_ROPE
DIL_W = DIL_HEADS * DIL_HEAD_DIM
LOG2E = math.log2(math.e)
LN2 = math.log(2.0)

LANES = 128
SUBLANES = 8
VMEM_LIMIT = 56 * 1024 * 1024

TM_PROJ = 512
SUB_PROJ = 256
CONV_HALO = 16
MLA_SCORE_ELEMS = 1 << 20
TQ_DIL = 256
DIL_ROWS_PER_STEP = 2048
TM_MERGE = 256
MERGE_CHUNK = 1024
TM_FFN = 1024
FFN_HALO = 16
FFN_CHUNK = 256


def _const_spec(shape):
    nd = len(shape)
    return pl.BlockSpec(shape, lambda *_: (0,) * nd, pipeline_mode=pl.Buffered(1))


def _sigmoid(x):
    return 1.0 / (1.0 + jnp.exp(-x))


def _rms_scale(x, n):
    return lax.rsqrt(jnp.sum(x * x, axis=-1, keepdims=True) * (1.0 / n) + EPS)


def _layer_norm(x, g, b):
    mu = jnp.mean(x, axis=-1, keepdims=True)
    xc = x - mu
    var = jnp.mean(xc * xc, axis=-1, keepdims=True)
    return xc * lax.rsqrt(var + EPS) * g + b


def _rope(x, cos, sin):
    return x * cos + pltpu.roll(x, LANES // 2, 1) * sin


def _inproj_kernel(x_ref, mcos_ref, msin_ref, dcos_ref, dsin_ref,
                   g_attn_ref, w_ab_ref, w_c_ref, w_d_ref,
                   sgu_g_ref, sgu_b_ref, ws_ref, bs_ref,
                   g_cq_ref, g_ckv_ref, w_uq_ref, w_uk_ref, w_uv_ref, gq_mla_ref, gk_mla_ref,
                   gq_dil_ref, gk_dil_ref,
                   ha_ref, yb_ref, qm_ref, km_ref, vm_ref, *dil_and_scratch):
    dil_refs, slab_ref = dil_and_scratch[:3 * N_DIL], dil_and_scratch[3 * N_DIL]
    tm = x_ref.shape[0]
    x = x_ref[...]
    h = (x * _rms_scale(x, D_MODEL) * g_attn_ref[...]).astype(BF16)

    dcos, dsin = dcos_ref[...], dsin_ref[...]
    lane_head = (lax.broadcasted_iota(jnp.int32, (tm, LANES), 1) % 64) // 16
    head_masks = [lane_head == hd for hd in range(DIL_HEADS)]
    slab_i = 0
    for which, gain_ref in enumerate((gq_dil_ref, gk_dil_ref, None)):
        for g in range(N_DIL):
            base = (which * N_DIL + g) * DIL_W
            z = jnp.dot(h, w_d_ref[:, base:base + DIL_W], preferred_element_type=F32)
            x0, x1 = z[:, :LANES], z[:, LANES:]
            if gain_ref is not None:
                sq = x0 * x0 + x1 * x1
                ms = jnp.zeros((tm, LANES), F32)
                for hd in range(DIL_HEADS):
                    ms = jnp.where(head_masks[hd],
                                   jnp.sum(jnp.where(head_masks[hd], sq, 0.0), axis=-1, keepdims=True), ms)
                scale = lax.rsqrt(ms * (1.0 / DIL_HEAD_DIM) + EPS)
                x0 = _rope(x0 * scale * gain_ref[:, :LANES], dcos, dsin)
                x1 = x1 * scale * gain_ref[:, LANES:]
            out_ref = dil_refs[which * N_DIL + g]
            d = out_ref.shape[0]
            for part, xc in enumerate((x0, x1)):
                lanes = slice(part * LANES, (part + 1) * LANES)
                if d == 1:
                    out_ref[0, :, lanes] = xc.astype(BF16)
                else:
                    slab = slab_ref.at[slab_i]
                    slab_i += 1
                    slab[...] = xc
                    for r in range(d):
                        out_ref[r, :, lanes] = slab[pl.ds(r, tm // d, stride=d), :].astype(BF16)

    z_c = jnp.dot(h, w_c_ref[...], preferred_element_type=F32)
    c_q = z_c[:, :MLA_Q_RANK]
    c_q = (c_q * _rms_scale(c_q, MLA_Q_RANK) * g_cq_ref[...]).astype(BF16)
    c_kv = z_c[:, MLA_Q_RANK:MLA_Q_RANK + MLA_KV_RANK]
    c_kv = (c_kv * _rms_scale(c_kv, MLA_KV_RANK) * g_ckv_ref[...]).astype(BF16)
    k_rope = z_c[:, MLA_Q_RANK + MLA_KV_RANK:]
    q = jnp.dot(c_q, w_uq_ref[...], preferred_element_type=F32)
    kn = jnp.dot(c_kv, w_uk_ref[...], preferred_element_type=F32)
    v_lane = lax.broadcasted_iota(jnp.int32, (tm, MLA_HEADS * LANES), 1) % LANES
    vm_ref[...] = jnp.where(v_lane == MLA_V, 1.0,
                            jnp.dot(c_kv, w_uv_ref[...], preferred_element_type=F32)).astype(BF16)
    mcos, msin = mcos_ref[...], msin_ref[...]
    for hd in range(MLA_HEADS):
        cols = slice(hd * LANES, (hd + 1) * LANES)
        qh = q[:, cols]
        qh = qh * _rms_scale(qh, MLA_DQK) * gq_mla_ref[...]
        qm_ref[:, cols] = _rope(qh, mcos, msin).astype(BF16)
        kh = kn[:, cols] + k_rope
        kh = kh * _rms_scale(kh, MLA_DQK) * gk_mla_ref[...]
        km_ref[:, cols] = _rope(kh, mcos, msin).astype(BF16)

    z_a = jnp.dot(h, w_ab_ref[:, :N_A], preferred_element_type=F32)
    ha_ref[...] = z_a[:, :CONV_CH] * _sigmoid(z_a[:, CONV_CH:])
    zb = jnp.dot(h, w_ab_ref[:, N_A:], preferred_element_type=F32)
    zb = 0.5 * zb * (1.0 + jnp.tanh(math.sqrt(2.0 / math.pi) * (zb + 0.044715 * (zb * zb * zb))))
    u = zb[:, :SGU_CH]
    v = _layer_norm(zb[:, SGU_CH:], sgu_g_ref[...], sgu_b_ref[...]).astype(BF16)
    lane = lax.broadcasted_iota(jnp.int32, (SGU_CHUNK, SGU_CH), 1)
    gw = SGU_CH // SGU_GROUPS
    for c in range(tm // SGU_CHUNK):
        rows = slice(c * SGU_CHUNK, (c + 1) * SGU_CHUNK)
        vc = v[rows]
        sv = jnp.dot(ws_ref[SGU_GROUPS - 1], vc, preferred_element_type=F32)
        for g in range(SGU_GROUPS - 2, -1, -1):
            sv = jnp.where(lane < (g + 1) * gw, jnp.dot(ws_ref[g], vc, preferred_element_type=F32), sv)
        yb_ref[rows, :] = (u[rows] * (sv + bs_ref[...])).astype(BF16)


N_PROJ_TOK_OUT = 5
N_PROJ_SLABS = 3 * N_DIL * DIL_W // LANES


def _inproj_tile_kernel(*refs, n_rowwise_in, n_weights):
    rowwise_in = refs[:n_rowwise_in]
    weights = refs[n_rowwise_in:n_rowwise_in + n_weights]
    outs = refs[n_rowwise_in + n_weights:]
    tok_outs, dil_outs, slab_ref = outs[:N_PROJ_TOK_OUT], outs[N_PROJ_TOK_OUT:-1], outs[-1]
    for sub in range(rowwise_in[0].shape[0] // SUB_PROJ):
        rows = pl.ds(sub * SUB_PROJ, SUB_PROJ)
        dil_views = []
        for r in dil_outs:
            per_class = SUB_PROJ // r.shape[0]
            dil_views.append(r.at[:, pl.ds(sub * per_class, per_class)])
        _inproj_kernel(*[r.at[rows] for r in rowwise_in], *weights, *[r.at[rows] for r in tok_outs], *dil_views,
                       slab_ref.at[pl.ds(sub * N_PROJ_SLABS, N_PROJ_SLABS)])


def _inproj(x, tabs, lw):
    B, S, _ = x.shape
    tm = TM_PROJ
    grid = (B, S // tm)
    tok = lambda w, dt: jax.ShapeDtypeStruct((B, S, w), dt)
    tok_spec = lambda w: pl.BlockSpec((None, tm, w), lambda b, i: (b, i, 0))
    tab_spec = pl.BlockSpec((tm, LANES), lambda b, i: (i, 0))
    weights = (lw['g_attn'], lw['w_ab'], lw['w_c'], lw['w_d'], lw['sgu_g'], lw['sgu_b'], lw['ws'], lw['bs'],
               lw['g_cq'], lw['g_ckv'], lw['w_uq'], lw['w_uk'], lw['w_uv'], lw['gq_mla'], lw['gk_mla'],
               lw['gq_dil'], lw['gk_dil'])
    dils = [d for _ in range(3) for (_, d) in DIL_GROUPS]
    outs = pl.pallas_call(
        functools.partial(_inproj_tile_kernel, n_rowwise_in=1 + len(tabs), n_weights=len(weights)),
        grid=grid,
        in_specs=[tok_spec(D_MODEL)] + [tab_spec] * len(tabs) + [_const_spec(w.shape) for w in weights],
        out_specs=[tok_spec(CONV_CH), tok_spec(SGU_CH), tok_spec(4 * LANES), tok_spec(4 * LANES), tok_spec(4 * LANES)]
                  + [pl.BlockSpec((None, d, tm // d, DIL_W), lambda b, i: (b, 0, i, 0)) for d in dils],
        out_shape=[tok(CONV_CH, F32), tok(SGU_CH, BF16), tok(4 * LANES, BF16), tok(4 * LANES, BF16), tok(4 * LANES, BF16)]
                  + [jax.ShapeDtypeStruct((B, d, S // d, DIL_W), BF16) for d in dils],
        scratch_shapes=[pltpu.VMEM((tm // SUB_PROJ * N_PROJ_SLABS, SUB_PROJ, LANES), F32)],
        compiler_params=pltpu.CompilerParams(dimension_semantics=("parallel", "parallel"),
                                             vmem_limit_bytes=VMEM_LIMIT),
        name="inproj",
    )(x, *tabs, *weights)
    return outs[:5], outs[5:5 + N_DIL], outs[5 + N_DIL:5 + 2 * N_DIL], outs[5 + 2 * N_DIL:]


def _conv_module_tile(prev_ref, cur_ref, next_ref, w_ref, b_ref, g_ref, beta_ref):
    tm = cur_ref.shape[0]
    i = pl.program_id(1)
    last = pl.num_programs(1) - 1
    ext = jnp.concatenate([jnp.where(i > 0, prev_ref[...], 0.0), cur_ref[...],
                           jnp.where(i < last, next_ref[...], 0.0)], axis=0)
    n_ext = ext.shape[0]
    shifted = [ext] + [pltpu.roll(ext, n_ext - s, 0) for s in range(1, SUBLANES)]
    acc = jnp.zeros((tm, CONV_CH), F32) + b_ref[...]
    base = CONV_HALO - CONV_WIDTH // 2
    for k in range(CONV_WIDTH):
        a, s = divmod(base + k, SUBLANES)
        acc = acc + shifted[s][a * SUBLANES:a * SUBLANES + tm, :] * w_ref[k:k + 1, :]
    y = _layer_norm(acc, g_ref[...], beta_ref[...])
    return (y * _sigmoid(y)).astype(BF16)


def _halo_specs(tm, halo, width, seq):
    nh = tm // halo
    n_halo_blocks = seq // halo
    return [pl.BlockSpec((None, halo, width), lambda b, i: (b, jnp.maximum(i * nh - 1, 0), 0)),
            pl.BlockSpec((None, tm, width), lambda b, i: (b, i, 0)),
            pl.BlockSpec((None, halo, width), lambda b, i: (b, jnp.minimum((i + 1) * nh, n_halo_blocks - 1), 0))]


def _mla_kernel(q_ref, k_ref, v_ref, o_ref):
    scores = []
    for hd in range(MLA_HEADS):
        cols = slice(hd * LANES, (hd + 1) * LANES)
        scores.append(lax.dot_general(q_ref[:, cols], k_ref[:, cols], (((1,), (1,)), ((), ())),
                                      preferred_element_type=F32))
    for hd in range(MLA_HEADS):
        cols = slice(hd * LANES, (hd + 1) * LANES)
        s = scores[hd]
        p = jnp.exp2(s - jnp.max(s, axis=-1, keepdims=True))
        o = jnp.dot(p.astype(BF16), v_ref[:, cols], preferred_element_type=F32)
        l = o[:, MLA_V:MLA_V + 1]
        o_ref[:, cols] = (o / l).astype(BF16)


def _mla_attention(q, k, v):
    B, S, W = q.shape
    tq = min(S, max(LANES, MLA_SCORE_ELEMS // S))
    return pl.pallas_call(
        _mla_kernel,
        grid=(B, S // tq),
        in_specs=[pl.BlockSpec((None, tq, W), lambda b, i: (b, i, 0)),
                  pl.BlockSpec((None, S, W), lambda b, i: (b, 0, 0)),
                  pl.BlockSpec((None, S, W), lambda b, i: (b, 0, 0))],
        out_specs=pl.BlockSpec((None, tq, W), lambda b, i: (b, i, 0)),
        out_shape=jax.ShapeDtypeStruct((B, S, W), BF16),
        compiler_params=pltpu.CompilerParams(dimension_semantics=("parallel", "arbitrary"),
                                             vmem_limit_bytes=VMEM_LIMIT),
        name="mla_attention",
    )(q, k, v)


def _dil_kernel(q_ref, k_ref, v_ref, o_ref, lse_ref, *, tq, tk):
    rb, tqb, _ = q_ref.shape
    n = k_ref.shape[1]
    i_base = pl.program_id(2) * tqb
    lane = lax.broadcasted_iota(jnp.int32, (tq, DIL_W), 1)
    q_head = (lane % 64) // 16
    low_half = lax.broadcasted_iota(jnp.int32, (tq, LANES), 1) < DIL_HEAD_DIM
    for r in range(rb):
        for t in range(tqb // tq):
            i0 = i_base + t * tq
            rows = slice(t * tq, (t + 1) * tq)
            if tk == n:
                start = 0
                kw, vw = k_ref[r], v_ref[r]
            else:
                start = pl.multiple_of(jnp.clip(i0 - DIL_SIDE, 0, n - tk), DIL_SIDE)
                kw, vw = k_ref[r, pl.ds(start, tk), :], v_ref[r, pl.ds(start, tk), :]
            q = q_ref[r, rows, :]
            row = i0 + lax.broadcasted_iota(jnp.int32, (tq, tk), 0)
            col = start + lax.broadcasted_iota(jnp.int32, (tq, tk), 1)
            valid = jnp.abs(row - col) <= DIL_SIDE
            outs, lses = [], []
            for hd in range(DIL_HEADS):
                qh = jnp.where(q_head == hd, q, jnp.zeros_like(q))
                s = lax.dot_general(qh, kw, (((1,), (1,)), ((), ())), preferred_element_type=F32)
                s = jnp.where(valid, s, NEG_INF)
                m = jnp.max(s, axis=-1, keepdims=True)
                p = jnp.exp2(s - m)
                l = jnp.sum(p, axis=-1, keepdims=True)
                col = slice((hd // 2) * LANES, (hd // 2 + 1) * LANES)
                o = jnp.dot(p.astype(BF16), vw, preferred_element_type=F32)
                outs.append(o[:, col] / l)
                lses.append(m * LN2 + jnp.log(l))
            for c in range(DIL_W // LANES):
                col = slice(c * LANES, (c + 1) * LANES)
                o_ref[r, rows, col] = jnp.where(low_half, outs[2 * c], outs[2 * c + 1])
                lse_ref[r, rows, col] = jnp.where(low_half, lses[2 * c], lses[2 * c + 1])


def _dil_group(q, k, v, g):
    B, d, n, _ = q.shape
    tq = min(TQ_DIL, n)
    tk = min(n, tq + 2 * DIL_SIDE)
    tqb = min(DIL_ROWS_PER_STEP, n)
    rb = min(d, max(1, DIL_ROWS_PER_STEP // tqb))
    q_spec = pl.BlockSpec((None, rb, tqb, DIL_W), lambda b, r, i: (b, r, i, 0))
    kv_spec = pl.BlockSpec((None, rb, n, DIL_W), lambda b, r, i: (b, r, 0, 0))
    return pl.pallas_call(
        functools.partial(_dil_kernel, tq=tq, tk=tk),
        grid=(B, d // rb, n // tqb),
        in_specs=[q_spec, kv_spec, kv_spec],
        out_specs=[q_spec, q_spec],
        out_shape=[jax.ShapeDtypeStruct((B, d, n, DIL_W), F32)] * 2,
        compiler_params=pltpu.CompilerParams(dimension_semantics=("parallel", "parallel", "arbitrary"),
                                             vmem_limit_bytes=VMEM_LIMIT),
        name=f"dil_attention_g{g}",
    )(q, k, v)


def _merge_kernel(x_ref, ha_prev_ref, ha_ref, ha_next_ref, yb_ref, om_ref,
                  o0_ref, l0_ref, o1_ref, l1_ref, o2_ref, l2_ref,
                  conv_w_ref, conv_b_ref, conv_g_ref, conv_beta_ref,
                  g_attn_ref, w_g_ref, woa_ref, wob_ref, woc_ref, wod_ref, w_out_ref, y_ref,
                  slab_ref, merged_ref):
    x = x_ref[...]
    tm = x.shape[0]
    h = (x * _rms_scale(x, D_MODEL) * g_attn_ref[...]).astype(BF16)

    def natural(ref, base):
        d = ref.shape[0]
        if d == 1:
            return ref[0]
        parts = []
        for part in range(DIL_W // LANES):
            slab = slab_ref.at[base + part]
            for r in range(d):
                slab[pl.ds(r, tm // d, stride=d), :] = ref[r, :, part * LANES:(part + 1) * LANES]
            parts.append(slab[...])
        return jnp.concatenate(parts, axis=1)

    o0, l0 = natural(o0_ref, 0), natural(l0_ref, 2)
    o1, l1 = natural(o1_ref, 4), natural(l1_ref, 6)
    o2, l2 = natural(o2_ref, 8), natural(l2_ref, 10)
    m = jnp.maximum(jnp.maximum(l0, l1), l2)
    e0, e1, e2 = jnp.exp(l0 - m), jnp.exp(l1 - m), jnp.exp(l2 - m)
    od = ((e0 * o0 + e1 * o1 + e2 * o2) / (e0 + e1 + e2)).astype(BF16)
    ca = _conv_module_tile(ha_prev_ref, ha_ref, ha_next_ref, conv_w_ref, conv_b_ref, conv_g_ref, conv_beta_ref)
    branches = ((ca, woa_ref), (yb_ref[...], wob_ref), (om_ref[...], woc_ref), (od, wod_ref))
    for c in range(D_MODEL // MERGE_CHUNK):
        cols = slice(c * MERGE_CHUNK, (c + 1) * MERGE_CHUNK)
        part = jnp.zeros((tm, MERGE_CHUNK), F32)
        for i, (act, wo_ref) in enumerate(branches):
            gcols = slice(i * D_MODEL + c * MERGE_CHUNK, i * D_MODEL + (c + 1) * MERGE_CHUNK)
            gate = _sigmoid(jnp.dot(h, w_g_ref[:, gcols], preferred_element_type=F32))
            part = part + gate * jnp.dot(act, wo_ref[:, cols], preferred_element_type=F32)
        merged_ref[:, cols] = part.astype(BF16)
    y_ref[...] = x + jnp.dot(merged_ref[...], w_out_ref[...], preferred_element_type=F32)


def _merge(x, ha, yb, om, dil, lw):
    B, S, _ = x.shape
    tm = TM_MERGE
    tok_spec = lambda w: pl.BlockSpec((None, tm, w), lambda b, i: (b, i, 0))
    dil_acts = tuple(a for pair in dil for a in pair)
    dil_spec = lambda d: pl.BlockSpec((None, d, tm // d, DIL_W), lambda b, i: (b, 0, i, 0))
    weights = (lw['conv_w'], lw['conv_b'], lw['conv_g'], lw['conv_beta'],
               lw['g_attn'], lw['w_g'], lw['wo_a'], lw['wo_b'], lw['wo_c'], lw['wo_d'], lw['w_out'])
    return pl.pallas_call(
        _merge_kernel,
        grid=(B, S // tm),
        in_specs=[tok_spec(D_MODEL)] + _halo_specs(tm, CONV_HALO, CONV_CH, S) + [tok_spec(SGU_CH), tok_spec(om.shape[-1])]
                 + [dil_spec(a.shape[1]) for a in dil_acts] + [_const_spec(w.shape) for w in weights],
        out_specs=tok_spec(D_MODEL),
        out_shape=jax.ShapeDtypeStruct((B, S, D_MODEL), F32),
        scratch_shapes=[pltpu.VMEM((2 * N_DIL * DIL_W // LANES, tm, LANES), F32),
                        pltpu.VMEM((tm, D_MODEL), BF16)],
        compiler_params=pltpu.CompilerParams(dimension_semantics=("parallel", "parallel"),
                                             vmem_limit_bytes=VMEM_LIMIT),
        name="merge",
    )(x, ha, ha, ha, yb, om, *dil_acts, *weights)


def _ffn_kernel(prev_ref, cur_ref, next_ref, g_ref, w_up_ref, cw_ref, cb_ref, wd_ref,
                y_ref, h_ref, act_ref):
    tm = cur_ref.shape[0]
    i = pl.program_id(1)
    last = pl.num_programs(1) - 1

    def normed(x):
        return (x * _rms_scale(x, D_MODEL) * g_ref[...]).astype(BF16)

    h_ref[0:FFN_HALO, :] = jnp.where(i > 0, normed(prev_ref[...]), jnp.zeros((FFN_HALO, D_MODEL), BF16))
    h_ref[FFN_HALO:FFN_HALO + tm, :] = normed(cur_ref[...])
    h_ref[FFN_HALO + tm:, :] = jnp.where(i < last, normed(next_ref[...]), jnp.zeros((FFN_HALO, D_MODEL), BF16))

    rows = slice(FFN_HALO, FFN_HALO + tm)
    n_ext = tm + 2 * FFN_HALO

    for j in range(w_up_ref.shape[0]):
        u = jnp.dot(h_ref[...], w_up_ref[j], preferred_element_type=F32)
        cw = cw_ref[j]
        c = (pltpu.roll(u, 1, 0)[rows] * cw[0:1, :] + u[rows] * cw[1:2, :]
             + pltpu.roll(u, n_ext - 1, 0)[rows] * cw[2:3, :] + cb_ref[j])
        a, b = c[:, :FFN_CHUNK], c[:, FFN_CHUNK:]
        act_ref[:, j * FFN_CHUNK:(j + 1) * FFN_CHUNK] = (a * _sigmoid(a) * b).astype(BF16)
    y_ref[...] = cur_ref[...] + jnp.dot(act_ref[...], wd_ref[...], preferred_element_type=F32)


def _ffn(x, lw):
    B, S, _ = x.shape
    tm = min(TM_FFN, S)
    weights = (lw['g_ffn'], lw['w_up'], lw['cw'], lw['cb'], lw['w_down'])
    return pl.pallas_call(
        _ffn_kernel,
        grid=(B, S // tm),
        in_specs=_halo_specs(tm, FFN_HALO, D_MODEL, S) + [_const_spec(w.shape) for w in weights],
        out_specs=pl.BlockSpec((None, tm, D_MODEL), lambda b, i: (b, i, 0)),
        out_shape=jax.ShapeDtypeStruct((B, S, D_MODEL), F32),
        scratch_shapes=[pltpu.VMEM((tm + 2 * FFN_HALO, D_MODEL), BF16),
                        pltpu.VMEM((tm, D_FF), BF16)],
        compiler_params=pltpu.CompilerParams(dimension_semantics=("parallel", "parallel"),
                                             vmem_limit_bytes=VMEM_LIMIT),
        name="conv_ffn",
    )(x, x, x, *weights)


def _mla_lane_map():
    half = MLA_ROPE // 2
    dim = -np.ones(LANES, np.int64)
    freq = -np.ones(LANES, np.int64)
    sign = np.zeros(LANES, np.float32)
    nope_lanes = list(range(half, LANES // 2)) + list(range(LANES // 2 + half, MLA_DQK))
    dim[nope_lanes] = np.arange(MLA_NOPE)
    for i in range(half):
        dim[i], freq[i], sign[i] = MLA_NOPE + i, i, -1.0
        dim[LANES // 2 + i], freq[LANES // 2 + i], sign[LANES // 2 + i] = MLA_NOPE + half + i, i, 1.0
    return dim, freq, sign


def _dil_lane_map():
    half = ROPE_DIMS // 2
    src = np.zeros(DIL_W, np.int64)
    freq = -np.ones(DIL_W, np.int64)
    sign = np.zeros(DIL_W, np.float32)
    for lane in range(DIL_W):
        col, l = divmod(lane, LANES)
        blk, hd, i = l // 64, (l % 64) // 16, l % 16
        if col == 0 and i < half:
            d = blk * half + i
            freq[lane], sign[lane] = i, (-1.0 if blk == 0 else 1.0)
        elif col == 0:
            d = ROPE_DIMS + blk * half + (i - half)
        else:
            d = 2 * ROPE_DIMS + blk * 16 + i
        src[lane] = hd * DIL_HEAD_DIM + d
    return src, freq, sign


def _rope_tables(seq, dims, theta, freq, sign):
    inv = jnp.exp(-math.log(theta) * jnp.arange(0, dims, 2, dtype=F32) / dims)
    ang = jnp.arange(seq, dtype=F32)[:, None] * inv[None, :]
    cos, sin = jnp.cos(ang), jnp.sin(ang)
    on = jnp.asarray(freq >= 0)[None, :]
    idx = np.maximum(freq, 0)
    return (jnp.where(on, cos[:, idx], 1.0), jnp.where(on, sin[:, idx] * jnp.asarray(sign)[None, :], 0.0))


def _place_lanes(w, dim):
    return jnp.where(jnp.asarray(dim >= 0)[None, :], w[:, np.maximum(dim, 0)], 0.0)


def _pad_heads(w, n_heads, width):
    k = w.shape[0]
    w = w.reshape(k, n_heads, width)
    return jnp.pad(w, ((0, 0), (0, 0), (0, LANES - width))).reshape(k, n_heads * LANES)


def _prep_layer(p, l):
    row = lambda a: a.reshape(1, -1).astype(F32)
    w_in = p['w_in'][l]
    o_c = N_A + N_B
    o_d = o_c + N_C
    o_g = o_d + N_D
    w_c = w_in[:, o_c:o_d]
    m_dim, _, _ = _mla_lane_map()
    nope_dim = np.where(m_dim < MLA_NOPE, m_dim, -1)
    rope_dim = np.where(m_dim >= MLA_NOPE, m_dim - MLA_NOPE, -1)
    w_kr = _place_lanes(w_c[:, MLA_Q_RANK + MLA_KV_RANK:], rope_dim)
    w_ukv = p['mla_w_ukv'][l].reshape(MLA_KV_RANK, MLA_HEADS, MLA_NOPE + MLA_V)
    w_uq = p['mla_w_uq'][l].reshape(MLA_Q_RANK, MLA_HEADS, MLA_DQK)
    per_head = lambda f: jnp.concatenate([f(hd) for hd in range(MLA_HEADS)], axis=1)
    d_src, _, _ = _dil_lane_map()
    w_d = w_in[:, o_d:o_g]
    n_qk = N_DIL * DIL_W
    qk_cols = np.concatenate([which * n_qk + g * DIL_W + d_src for which in range(2) for g in range(N_DIL)])
    n_chunks = D_FF // FFN_CHUNK
    w_up = p['ffn_w_up'][l]
    chunk_cols = lambda w: w.reshape(w.shape[0], n_chunks, FFN_CHUNK).transpose(1, 0, 2)
    pair_chunks = lambda w: jnp.concatenate([chunk_cols(w[:, :D_FF]), chunk_cols(w[:, D_FF:])], axis=2)
    cw, cb = p['ffn_conv_w'][l], p['ffn_conv_b'][l].reshape(1, -1)
    return {
        'g_attn': row(p['attn_norm'][l]),
        'w_ab': w_in[:, :o_c].astype(BF16),
        'w_c': jnp.concatenate([w_c[:, :MLA_Q_RANK + MLA_KV_RANK], w_kr], axis=1).astype(BF16),
        'w_d': jnp.concatenate([w_d[:, qk_cols], w_d[:, 2 * n_qk:]], axis=1).astype(BF16),
        'w_g': w_in[:, o_g:].astype(BF16),
        'conv_w': p['conv_w'][l], 'conv_b': row(p['conv_b'][l]),
        'conv_g': row(p['conv_ln_g'][l]), 'conv_beta': row(p['conv_ln_b'][l]),
        'sgu_g': row(p['sgu_ln_g'][l]), 'sgu_b': row(p['sgu_ln_b'][l]),
        'ws': p['sgu_w_s'][l].astype(BF16),
        'bs': jnp.repeat(p['sgu_b_s'][l].T, SGU_CH // SGU_GROUPS, axis=1),
        'g_cq': row(p['mla_g_cq'][l]), 'g_ckv': row(p['mla_g_ckv'][l]),
        'w_uq': per_head(lambda hd: _place_lanes(w_uq[:, hd], m_dim)).astype(BF16),
        'w_uk': per_head(lambda hd: _place_lanes(w_ukv[:, hd, :MLA_NOPE], nope_dim)).astype(BF16),
        'w_uv': _pad_heads(w_ukv[:, :, MLA_NOPE:].reshape(MLA_KV_RANK, -1), MLA_HEADS, MLA_V).astype(BF16),
        'gq_mla': _place_lanes(row(p['mla_g_qn'][l]) * (MLA_DQK ** -0.5 * LOG2E), m_dim),
        'gk_mla': _place_lanes(row(p['mla_g_kn'][l]), m_dim),
        'gq_dil': row(p['dil_g_qn'][l])[:, d_src % DIL_HEAD_DIM] * (DIL_HEAD_DIM ** -0.5 * LOG2E),
        'gk_dil': row(p['dil_g_kn'][l])[:, d_src % DIL_HEAD_DIM],
        'wo_a': p['conv_w_o'][l].astype(BF16), 'wo_b': p['sgu_w_o'][l].astype(BF16),
        'wo_c': _pad_heads(p['mla_w_o'][l].T, MLA_HEADS, MLA_V).T.astype(BF16),
        'wo_d': p['dil_w_o'][l].astype(BF16),
        'w_out': p['w_out'][l].astype(BF16),
        'g_ffn': row(p['ffn_norm'][l]),
        'w_up': pair_chunks(w_up).astype(BF16), 'cw': pair_chunks(cw), 'cb': pair_chunks(cb),
        'w_down': p['ffn_w_down'][l].astype(BF16),
    }


def _trunk(x, layers):
    S = x.shape[1]
    _, m_freq, m_sign = _mla_lane_map()
    _, d_freq, d_sign = _dil_lane_map()
    tabs = (_rope_tables(S, MLA_ROPE, MLA_THETA, m_freq, m_sign)
            + _rope_tables(S, ROPE_DIMS, ROPE_THETA, d_freq[:LANES], d_sign[:LANES]))
    for lw in layers:
        (ha, yb, qm, km, vm), qd, kd, vd = _inproj(x, tabs, lw)
        om = _mla_attention(qm, km, vm)
        dil = [_dil_group(qd[g], kd[g], vd[g], g) for g in range(N_DIL)]
        x = _merge(x, ha, yb, om, dil, lw)
        x = _ffn(x, lw)
    return x


def kernel(x_prompt, x_sample, attn_norm, w_in, conv_w, conv_b, conv_ln_g, conv_ln_b, conv_w_o, sgu_ln_g, sgu_ln_b, sgu_w_s, sgu_b_s, sgu_w_o, mla_g_cq, mla_g_ckv, mla_w_uq, mla_w_ukv, mla_g_qn, mla_g_kn, mla_w_o, dil_g_qn, dil_g_kn, dil_w_o, w_out, ffn_norm, ffn_w_up, ffn_conv_w, ffn_conv_b, ffn_w_down):
    p = dict(attn_norm=attn_norm, w_in=w_in, conv_w=conv_w, conv_b=conv_b, conv_ln_g=conv_ln_g, conv_ln_b=conv_ln_b,
             conv_w_o=conv_w_o, sgu_ln_g=sgu_ln_g, sgu_ln_b=sgu_ln_b, sgu_w_s=sgu_w_s, sgu_b_s=sgu_b_s,
             sgu_w_o=sgu_w_o, mla_g_cq=mla_g_cq, mla_g_ckv=mla_g_ckv, mla_w_uq=mla_w_uq, mla_w_ukv=mla_w_ukv,
             mla_g_qn=mla_g_qn, mla_g_kn=mla_g_kn, mla_w_o=mla_w_o, dil_g_qn=dil_g_qn, dil_g_kn=dil_g_kn,
             dil_w_o=dil_w_o, w_out=w_out, ffn_norm=ffn_norm, ffn_w_up=ffn_w_up, ffn_conv_w=ffn_conv_w,
             ffn_conv_b=ffn_conv_b, ffn_w_down=ffn_w_down)
    layers = [_prep_layer(p, l) for l in range(DEPTH)]
    return (_trunk(x_prompt, layers), _trunk(x_sample, layers))
```
